```python
import math
import jax, jax.numpy as jnp
from jax import lax
import numpy as np

D_MODEL = 1024
BATCH = 32
SEQ = 2048
DEPTH = 4

N_MIXERS = 3
N_SSD_LAYERS = (DEPTH + 2) // N_MIXERS
N_S5_LAYERS = (DEPTH + 1) // N_MIXERS
N_DSA_LAYERS = DEPTH // N_MIXERS

NORM_EPS = 1e-6
FFN_DIM = 2816

SSD_INNER = 2 * D_MODEL
SSD_HEAD_DIM = 64
SSD_HEADS = SSD_INNER // SSD_HEAD_DIM
SSD_GROUPS = 8
SSD_HEADS_PER_GROUP = SSD_HEADS // SSD_GROUPS
SSD_STATE = 128
SSD_CONV = 4
SSD_CONV_DIM = SSD_INNER + 2 * SSD_GROUPS * SSD_STATE
SSD_PROJ = SSD_INNER + SSD_CONV_DIM + SSD_HEADS
SCAN_CHUNK_MAX = 256

S5_GROUP_WIDTH = 16
S5_GROUPS = D_MODEL // S5_GROUP_WIDTH
S5_STATE = 64

ATTN_HEADS = 16
ATTN_HEAD_DIM = D_MODEL // ATTN_HEADS
IDX_HEADS = 8
IDX_DIM = 64
TOPK_MAX = 256
QUERY_BLOCK = 128
ROPE_THETA = 10000.0
DSA_Q = ATTN_HEADS * ATTN_HEAD_DIM
DSA_SPLITS = (DSA_Q,
              DSA_Q + ATTN_HEAD_DIM,
              DSA_Q + 2 * ATTN_HEAD_DIM,
              DSA_Q + 2 * ATTN_HEAD_DIM + IDX_HEADS * IDX_DIM,
              DSA_Q + 2 * ATTN_HEAD_DIM + IDX_HEADS * IDX_DIM + IDX_DIM)
DSA_PROJ = DSA_SPLITS[-1] + IDX_HEADS

kernel_name = "hybrid_ssd_s5_dsa_macaron"


def rmsnorm(x, g):
    xf = x.astype(jnp.float32)
    xf = xf * lax.rsqrt(jnp.mean(xf * xf, axis=-1, keepdims=True) + NORM_EPS)
    return xf.astype(x.dtype) * g


def swiglu(h, w_in, w_out):
    gate, up = jnp.split(h @ w_in, 2, axis=-1)
    return (jax.nn.silu(gate) * up) @ w_out


def rope(t, pos):
    d = t.shape[-1]
    half = d // 2
    inv = ROPE_THETA ** (-jnp.arange(half, dtype=jnp.float32) / half)
    ang = pos.astype(jnp.float32)[:, None] * inv
    ang = ang.reshape((1, pos.shape[0]) + (1,) * (t.ndim - 3) + (half,))
    cos, sin = jnp.cos(ang), jnp.sin(ang)
    tf = t.astype(jnp.float32)
    t1, t2 = tf[..., :half], tf[..., half:]
    return jnp.concatenate([t1 * cos - t2 * sin, t2 * cos + t1 * sin], axis=-1).astype(t.dtype)


def chunk_len(L):
    return math.gcd(L, SCAN_CHUNK_MAX)


def causal_depthwise_conv(u, w, b):
    K = w.shape[0]
    L = u.shape[1]
    up = jnp.pad(u, ((0, 0), (K - 1, 0), (0, 0)))
    out = up[:, 0:L] * w[0]
    for k in range(1, K):
        out = out + up[:, k:k + L] * w[k]
    return out + b


def ssd_chunked_scan(xs, dt, A, bm, cm):
    bsz, L = xs.shape[:2]
    Q = chunk_len(L)
    n_chunks = L // Q

    def to_chunks(t):
        return jnp.moveaxis(t.reshape((bsz, n_chunks, Q) + t.shape[2:]), 1, 0)

    causal = jnp.tril(jnp.ones((Q, Q), dtype=bool))

    def step(state, inp):
        xc, dtc, bc, cc = inp
        cs = jnp.cumsum(dtc * A, axis=1)
        seg = cs[:, :, None] - cs[:, None]
        decay = jnp.exp(jnp.where(causal[None, :, :, None, None], seg, -jnp.inf))
        cb = jnp.einsum('btgn,bsgn->btsg', cc, bc)
        dtx = dtc[..., None] * xc
        y_diag = jnp.einsum('btsgk,bsgkp->btgkp', cb[..., None] * decay, dtx)
        y_off = jnp.einsum('btgn,bgkpn->btgkp', cc, state) * jnp.exp(cs)[..., None]
        to_end = jnp.exp(cs[:, -1:] - cs)
        new_state = (state * jnp.exp(cs[:, -1])[..., None, None]
                     + jnp.einsum('bsgn,bsgkp->bgkpn', bc, to_end[..., None] * dtx))
        return new_state, y_diag + y_off

    state0 = jnp.zeros((bsz, SSD_GROUPS, SSD_HEADS_PER_GROUP, SSD_HEAD_DIM, SSD_STATE), jnp.float32)
    _, ys = lax.scan(step, state0, (to_chunks(xs), to_chunks(dt), to_chunks(bm), to_chunks(cm)))
    return jnp.moveaxis(ys, 0, 1).reshape(xs.shape)


def ssd_mixer(h, in_proj, conv_w, conv_b, dt_bias, a_log, d, gate_norm, out_proj):
    bsz, L, _ = h.shape
    f32 = jnp.float32
    z, xbc, dt = jnp.split(h @ in_proj, [SSD_INNER, SSD_INNER + SSD_CONV_DIM], axis=-1)
    xbc = jax.nn.silu(causal_depthwise_conv(xbc, conv_w, conv_b))
    xs, bm, cm = jnp.split(xbc, [SSD_INNER, SSD_INNER + SSD_GROUPS * SSD_STATE], axis=-1)
    xs = xs.astype(f32).reshape(bsz, L, SSD_GROUPS, SSD_HEADS_PER_GROUP, SSD_HEAD_DIM)
    bm = bm.astype(f32).reshape(bsz, L, SSD_GROUPS, SSD_STATE)
    cm = cm.astype(f32).reshape(bsz, L, SSD_GROUPS, SSD_STATE)
    dt = jax.nn.softplus(dt.astype(f32) + dt_bias.astype(f32)).reshape(bsz, L, SSD_GROUPS, SSD_HEADS_PER_GROUP)
    A = -jnp.exp(a_log.astype(f32)).reshape(SSD_GROUPS, SSD_HEADS_PER_GROUP)
    y = ssd_chunked_scan(xs, dt, A, bm, cm)
    y = y + d.astype(f32).reshape(SSD_GROUPS, SSD_HEADS_PER_GROUP, 1) * xs
    y = y.reshape(bsz, L, SSD_GROUPS, -1) * jax.nn.silu(z.astype(f32)).reshape(bsz, L, SSD_GROUPS, -1)
    y = y * lax.rsqrt(jnp.mean(y * y, axis=-1, keepdims=True) + NORM_EPS)
    y = y.reshape(bsz, L, SSD_INNER).astype(h.dtype) * gate_norm
    return y @ out_proj


def s5_binop(e1, e2):
    a1, b1 = e1
    a2, b2 = e2
    return a1 * a2, a2 * b1 + b2


def s5_mixer(h, b_re, b_im, c_re, c_im, lam_re, lam_im, log_step, d, glu_w, glu_b):
    bsz, L, dm = h.shape
    f32 = jnp.float32
    lam = lax.complex(lam_re.astype(f32), lam_im.astype(f32))
    step = jnp.exp(log_step.astype(f32))[:, None]
    lam_bar = jnp.exp(lam * step)
    b_bar = ((lam_bar - 1.0) / lam)[..., None] * lax.complex(b_re.astype(f32), b_im.astype(f32))
    c = lax.complex(c_re.astype(f32), c_im.astype(f32))
    u = h.astype(f32).reshape(bsz, L, S5_GROUPS, S5_GROUP_WIDTH)
    Q = chunk_len(L)
    n_chunks = L // Q
    u_chunks = jnp.moveaxis(u.reshape(bsz, n_chunks, Q, S5_GROUPS, S5_GROUP_WIDTH), 1, 0)
    a = jnp.broadcast_to(lam_bar, (1, Q, S5_GROUPS, S5_STATE))

    def step_fn(h0, uc):
        bu = jnp.einsum('bqgc,gpc->bqgp', uc.astype(jnp.complex64), b_bar)
        bu = bu.at[:, 0].add(lam_bar * h0)
        _, states = lax.associative_scan(s5_binop, (a, bu), axis=1)
        yc = jnp.einsum('gcp,bqgp->bqgc', c, states).real
        return states[:, -1], yc

    h0 = jnp.zeros((bsz, S5_GROUPS, S5_STATE), jnp.complex64)
    _, ys = lax.scan(step_fn, h0, u_chunks)
    y = jnp.moveaxis(ys, 0, 1).reshape(bsz, L, dm) + d.astype(f32) * u.reshape(bsz, L, dm)
    g = jax.nn.gelu(y).astype(h.dtype)
    val, gate = jnp.split(g @ glu_w + glu_b, 2, axis=-1)
    return val * jax.nn.sigmoid(gate)


def dsa_mixer(h, w_in, w_out):
    bsz, L, _ = h.shape
    f32 = jnp.float32
    q, k, v, qi, ki, wi = jnp.split(h @ w_in, list(DSA_SPLITS), axis=-1)
    pos = jnp.arange(L)
    q = rope(q.reshape(bsz, L, ATTN_HEADS, ATTN_HEAD_DIM), pos)
    k = rope(k, pos)
    qi = rope(qi.reshape(bsz, L, IDX_HEADS, IDX_DIM), pos)
    ki = rope(ki, pos)
    wi = wi * (IDX_HEADS ** -0.5 * IDX_DIM ** -0.5)
    top_k = min(TOPK_MAX, L // 4)
    n_blocks = L // QUERY_BLOCK

    def blocks(t):
        return jnp.moveaxis(t.reshape((bsz, n_blocks, QUERY_BLOCK) + t.shape[2:]), 1, 0)

    def attend_block(inp):
        qb, qib, wb, tb = inp
        idx = jnp.einsum('bqhs,bqh->bqs', jax.nn.relu(jnp.einsum('bqhd,bsd->bqhs', qib, ki)), wb).astype(f32)
        idx = jnp.where(pos[None, None, :] <= tb[None, :, None], idx, -jnp.inf)
        _, sel = lax.top_k(idx, top_k)
        k_sel = jax.vmap(lambda kk, ii: kk[ii])(k, sel)
        v_sel = jax.vmap(lambda vv, ii: vv[ii])(v, sel)
        logits = jnp.einsum('bqhd,bqkd->bqhk', qb, k_sel).astype(f32) * (ATTN_HEAD_DIM ** -0.5)
        logits = jnp.where((sel <= tb[None, :, None])[:, :, None, :], logits, -jnp.inf)
        p = jax.nn.softmax(logits, axis=-1).astype(v.dtype)
        return jnp.einsum('bqhk,bqkd->bqhd', p, v_sel)

    o = lax.map(attend_block, (blocks(q), blocks(qi), blocks(wi), pos.reshape(n_blocks, QUERY_BLOCK)))
    o = jnp.moveaxis(o, 0, 1).reshape(bsz, L, ATTN_HEADS * ATTN_HEAD_DIM)
    return o @ w_out


def setup_inputs(seed: int = 0) -> dict:
    key = jax.random.key(seed)
    ks = iter(jax.random.split(key, 48))
    f32 = jnp.float32

    def nrm(shape, scale):
        return jax.random.normal(next(ks), shape, f32) * scale

    def gain(shape):
        return 1.0 + 0.02 * jax.random.normal(next(ks), shape, f32)

    x = nrm((BATCH, SEQ, D_MODEL), 1.0)
    ffn1_norm = gain((DEPTH, D_MODEL))
    ffn1_w_in = nrm((DEPTH, D_MODEL, 2 * FFN_DIM), D_MODEL ** -0.5)
    ffn1_w_out = nrm((DEPTH, FFN_DIM, D_MODEL), FFN_DIM ** -0.5)
    mix_norm = gain((DEPTH, D_MODEL))
    ffn2_norm = gain((DEPTH, D_MODEL))
    ffn2_w_in = nrm((DEPTH, D_MODEL, 2 * FFN_DIM), D_MODEL ** -0.5)
    ffn2_w_out = nrm((DEPTH, FFN_DIM, D_MODEL), FFN_DIM ** -0.5)

    nA = N_SSD_LAYERS
    ssd_in_proj = nrm((nA, D_MODEL, SSD_PROJ), D_MODEL ** -0.5)
    ssd_conv_w = nrm((nA, SSD_CONV, SSD_CONV_DIM), SSD_CONV ** -0.5)
    ssd_conv_b = nrm((nA, SSD_CONV_DIM), 0.01)
    dt0 = jnp.exp(jax.random.uniform(next(ks), (nA, SSD_HEADS), f32, math.log(1e-3), math.log(1e-1)))
    ssd_dt_bias = dt0 + jnp.log(-jnp.expm1(-dt0))
    ssd_a_log = jnp.log(jax.random.uniform(next(ks), (nA, SSD_HEADS), f32, 1.0, 16.0))
    ssd_d = gain((nA, SSD_HEADS))
    ssd_gate_norm = gain((nA, SSD_INNER))
    ssd_out_proj = nrm((nA, SSD_INNER, D_MODEL), SSD_INNER ** -0.5)

    nB = N_S5_LAYERS
    s5_b_re = nrm((nB, S5_GROUPS, S5_STATE, S5_GROUP_WIDTH), (2 * S5_GROUP_WIDTH) ** -0.5)
    s5_b_im = nrm((nB, S5_GROUPS, S5_STATE, S5_GROUP_WIDTH), (2 * S5_GROUP_WIDTH) ** -0.5)
    s5_c_re = nrm((nB, S5_GROUPS, S5_GROUP_WIDTH, S5_STATE), (2 * S5_STATE) ** -0.5)
    s5_c_im = nrm((nB, S5_GROUPS, S5_GROUP_WIDTH, S5_STATE), (2 * S5_STATE) ** -0.5)
    s5_lam_re = -0.5 + nrm((nB, S5_GROUPS, S5_STATE), 0.01)
    s5_lam_im = math.pi * jnp.arange(S5_STATE, dtype=f32) + nrm((nB, S5_GROUPS, S5_STATE), 0.01)
    s5_log_step = jax.random.uniform(next(ks), (nB, S5_GROUPS), f32, math.log(1e-3), math.log(1e-1))
    s5_d = nrm((nB, D_MODEL), 1.0)
    s5_glu_w = nrm((nB, D_MODEL, 2 * D_MODEL), D_MODEL ** -0.5)
    s5_glu_b = nrm((nB, 2 * D_MODEL), 0.01)

    nC = N_DSA_LAYERS
    dsa_in_proj = nrm((nC, D_MODEL, DSA_PROJ), D_MODEL ** -0.5)
    dsa_out_proj = nrm((nC, DSA_Q, D_MODEL), DSA_Q ** -0.5)

    final_norm = gain((D_MODEL,))
    return {"x": x, "ffn1_norm": ffn1_norm, "ffn1_w_in": ffn1_w_in, "ffn1_w_out": ffn1_w_out,
            "mix_norm": mix_norm, "ffn2_norm": ffn2_norm, "ffn2_w_in": ffn2_w_in, "ffn2_w_out": ffn2_w_out,
            "ssd_in_proj": ssd_in_proj, "ssd_conv_w": ssd_conv_w, "ssd_conv_b": ssd_conv_b,
            "ssd_dt_bias": ssd_dt_bias, "ssd_a_log": ssd_a_log, "ssd_d": ssd_d,
            "ssd_gate_norm": ssd_gate_norm, "ssd_out_proj": ssd_out_proj,
            "s5_b_re": s5_b_re, "s5_b_im": s5_b_im, "s5_c_re": s5_c_re, "s5_c_im": s5_c_im,
            "s5_lam_re": s5_lam_re, "s5_lam_im": s5_lam_im, "s5_log_step": s5_log_step,
            "s5_d": s5_d, "s5_glu_w": s5_glu_w, "s5_glu_b": s5_glu_b,
            "dsa_in_proj": dsa_in_proj, "dsa_out_proj": dsa_out_proj, "final_norm": final_norm}


def reference(x, ffn1_norm, ffn1_w_in, ffn1_w_out, mix_norm, ffn2_norm, ffn2_w_in, ffn2_w_out,
              ssd_in_proj, ssd_conv_w, ssd_conv_b, ssd_dt_bias, ssd_a_log, ssd_d, ssd_gate_norm, ssd_out_proj,
              s5_b_re, s5_b_im, s5_c_re, s5_c_im, s5_lam_re, s5_lam_im, s5_log_step, s5_d, s5_glu_w, s5_glu_b,
              dsa_in_proj, dsa_out_proj, final_norm):
    for i in range(DEPTH):
        j = i // N_MIXERS
        kind = i % N_MIXERS
        x = x + 0.5 * swiglu(rmsnorm(x, ffn1_norm[i]), ffn1_w_in[i], ffn1_w_out[i])
        h = rmsnorm(x, mix_norm[i])
        if kind == 0:
            m = ssd_mixer(h, ssd_in_proj[j], ssd_conv_w[j], ssd_conv_b[j], ssd_dt_bias[j],
                          ssd_a_log[j], ssd_d[j], ssd_gate_norm[j], ssd_out_proj[j])
        elif kind == 1:
            m = s5_mixer(h, s5_b_re[j], s5_b_im[j], s5_c_re[j], s5_c_im[j], s5_lam_re[j],
                         s5_lam_im[j], s5_log_step[j], s5_d[j], s5_glu_w[j], s5_glu_b[j])
        else:
            m = dsa_mixer(h, dsa_in_proj[j], dsa_out_proj[j])
        x = x + m
        x = x + 0.5 * swiglu(rmsnorm(x, ffn2_norm[i]), ffn2_w_in[i], ffn2_w_out[i])
    return rmsnorm(x, final_norm)
```

```python
import functools
import math

import jax
import jax.numpy as jnp
from jax import lax
from jax.experimental import pallas as pl
from jax.experimental.pallas import tpu as pltpu

F32 = jnp.float32
BF16 = jnp.bfloat16

NORM_EPS = 1e-6
N_MIXERS = 3

SSD_HEAD_DIM = 64
SSD_GROUPS = 8
SSD_HEADS_PER_GROUP = 4
SSD_HEADS = SSD_GROUPS * SSD_HEADS_PER_GROUP
SSD_STATE = 128
SSD_INNER = SSD_HEADS * SSD_HEAD_DIM
SSD_CONV = 4
SSD_CHUNK_MAX = 256
SSD_GROUP_WIDTH = SSD_HEADS_PER_GROUP * SSD_HEAD_DIM

S5_GROUP_WIDTH = 16
S5_STATE = 64
S5_SUB = 16

ATTN_HEADS = 16
ATTN_HEAD_DIM = 64
IDX_HEADS = 8
IDX_DIM = 64
TOPK_MAX = 256
QUERY_BLOCK = 128
ROPE_THETA = 10000.0

LANES = 128
SUBLANES = 8
VMEM_LIMIT_BYTES = 56 * 1024 * 1024

_NT = (((1,), (1,)), ((), ()))


def _params(*semantics):
    return pltpu.CompilerParams(dimension_semantics=semantics, vmem_limit_bytes=VMEM_LIMIT_BYTES)


def _rms(x, g):
    ms = jnp.mean(x * x, axis=-1, keepdims=True)
    return x * lax.rsqrt(ms + NORM_EPS) * g


def _silu(x):
    return x * jax.nn.sigmoid(x)


def _softplus(x):
    return jnp.maximum(x, 0.0) + jnp.log1p(jnp.exp(-jnp.abs(x)))


def _resident(shape):
    zeros = (0,) * len(shape)
    return pl.BlockSpec(shape, lambda *_: zeros, pipeline_mode=pl.Buffered(1))


def _ffn_kernel(x_ref, g_ref, wg_ref, wu_ref, wo_ref, pg_ref, o_ref, h_scr, acc_scr, *, post_norm):
    j = pl.program_id(1)

    @pl.when(j == 0)
    def _():
        h_scr[...] = _rms(x_ref[...], g_ref[...]).astype(BF16)
        acc_scr[...] = jnp.zeros_like(acc_scr)

    h = h_scr[...]
    gate = jnp.dot(h, wg_ref[...], preferred_element_type=F32)
    up = jnp.dot(h, wu_ref[...], preferred_element_type=F32)
    act = (_silu(gate) * up).astype(BF16)
    acc_scr[...] += jnp.dot(act, wo_ref[...], preferred_element_type=F32)

    @pl.when(j == pl.num_programs(1) - 1)
    def _():
        y = x_ref[...] + 0.5 * acc_scr[...]
        if post_norm:
            y = _rms(y, pg_ref[...])
        o_ref[...] = y


def _ffn_call(x, g, w_in, w_out, post_g=None):
    t, d = x.shape
    f = w_out.shape[0]
    tm = 512
    nj = 2
    tf = f // nj
    assert t % tm == 0 and f % nj == 0 and tf % LANES == 0
    post_norm = post_g is not None
    pg = post_g if post_norm else g
    return pl.pallas_call(
        functools.partial(_ffn_kernel, post_norm=post_norm),
        grid=(t // tm, nj),
        in_specs=[
            pl.BlockSpec((tm, d), lambda i, j: (i, 0)),
            pl.BlockSpec((1, d), lambda i, j: (0, 0)),
            pl.BlockSpec((d, tf), lambda i, j: (0, j)),
            pl.BlockSpec((d, tf), lambda i, j: (0, j + nj)),
            pl.BlockSpec((tf, d), lambda i, j: (j, 0)),
            pl.BlockSpec((1, d), lambda i, j: (0, 0)),
        ],
        out_specs=pl.BlockSpec((tm, d), lambda i, j: (i, 0)),
        out_shape=jax.ShapeDtypeStruct((t, d), F32),
        scratch_shapes=[pltpu.VMEM((tm, d), BF16), pltpu.VMEM((tm, d), F32)],
        compiler_params=_params("parallel", "arbitrary"),
        name="ffn",
    )(x, g.reshape(1, d), w_in, w_in, w_out, pg.reshape(1, d))


def _split3(a):
    hi = a.astype(BF16)
    r1 = a - hi.astype(F32)
    mid = r1.astype(BF16)
    lo = (r1 - mid.astype(F32)).astype(BF16)
    return hi, mid, lo


def _ssd_kernel(x_ref, gmix_ref, wz_ref, wxbc_ref, wdt_ref, wdtt_ref, convw_ref, convb_ref,
                dtbc_ref, dtbr_ref, ac_ref, ar_ref, dvec_ref, gn_ref, wout_ref,
                o_ref,
                xbc_scr, xs_scr, bm_scr, cm_scr, z_scr, yn_scr, state_scr, *, q):
    c = pl.program_id(1)
    pad = SUBLANES
    gw = SSD_GROUP_WIDTH
    hd_dim = SSD_HEAD_DIM
    n_state = SSD_STATE

    @pl.when(c == 0)
    def _():
        state_scr[...] = jnp.zeros_like(state_scr)
        xbc_scr[0:pad, :] = jnp.zeros((pad, xbc_scr.shape[1]), F32)

    @pl.when(c > 0)
    def _():
        xbc_scr[0:pad, :] = xbc_scr[q:q + pad, :]

    x = x_ref[...]
    h = _rms(x, gmix_ref[...]).astype(BF16)
    xbc_scr[pad:pad + q, :] = jnp.dot(h, wxbc_ref[...], preferred_element_type=F32)
    z_scr[...] = jnp.dot(h, wz_ref[...], preferred_element_type=F32)

    cw = 512
    n_x = SSD_INNER // cw
    n_b = SSD_GROUPS * n_state // cw
    for j in range(xbc_scr.shape[1] // cw):
        sl = slice(j * cw, (j + 1) * cw)
        w = convw_ref[:, sl]
        acc = xbc_scr[pad - 3:pad - 3 + q, sl] * w[0:1]
        acc = acc + xbc_scr[pad - 2:pad - 2 + q, sl] * w[1:2]
        acc = acc + xbc_scr[pad - 1:pad - 1 + q, sl] * w[2:3]
        acc = acc + xbc_scr[pad:pad + q, sl] * w[3:4]
        act = _silu(acc + convb_ref[:, sl])
        if j < n_x:
            xs_scr[:, sl] = act
        elif j < n_x + n_b:
            bm_scr[:, (j - n_x) * cw:(j - n_x + 1) * cw] = act
        else:
            cm_scr[:, (j - n_x - n_b) * cw:(j - n_x - n_b + 1) * cw] = act

    dt_c = _softplus(jnp.dot(h, wdt_ref[...], preferred_element_type=F32) + dtbc_ref[...])
    dt_r = _softplus(lax.dot_general(wdtt_ref[...], h, _NT, preferred_element_type=F32) + dtbr_ref[...])
    a_c = dt_c * ac_ref[...]
    a_r = dt_r * ar_ref[...]
    row = lax.broadcasted_iota(jnp.int32, (q, q), 0)
    col = lax.broadcasted_iota(jnp.int32, (q, q), 1)
    causal = col <= row
    tri_low = jnp.where(causal, 1.0, 0.0).astype(BF16)
    tri_up = jnp.where(row <= col, 1.0, 0.0).astype(BF16)
    cs_c = sum(jnp.dot(tri_low, p, preferred_element_type=F32) for p in _split3(a_c))
    cs_r = sum(jnp.dot(p, tri_up, preferred_element_type=F32) for p in _split3(a_r))
    e_c = jnp.exp(cs_c)
    f_c = dt_c * jnp.exp(cs_c[q - 1:q, :] - cs_c)

    for g in range(SSD_GROUPS):
        bg = bm_scr[:, g * n_state:(g + 1) * n_state].astype(BF16)
        cg = cm_scr[:, g * n_state:(g + 1) * n_state].astype(BF16)
        cb = lax.dot_general(cg, bg, _NT, preferred_element_type=F32)
        xs_g = xs_scr[:, g * gw:(g + 1) * gw]
        ys, ws = [], []
        for k in range(SSD_HEADS_PER_GROUP):
            hd = g * SSD_HEADS_PER_GROUP + k
            seg = cs_c[:, hd:hd + 1] - cs_r[hd:hd + 1, :]
            decay = jnp.exp(jnp.where(causal, seg, -jnp.inf))
            m = (cb * decay).astype(BF16)
            xk = xs_g[:, k * hd_dim:(k + 1) * hd_dim]
            dtx = dt_c[:, hd:hd + 1] * xk
            y = jnp.dot(m, dtx.astype(BF16), preferred_element_type=F32)
            st = state_scr[hd].astype(BF16)
            y = y + lax.dot_general(cg, st, _NT, preferred_element_type=F32) * e_c[:, hd:hd + 1]
            y = y + dvec_ref[:, hd * hd_dim:(hd + 1) * hd_dim] * xk
            ys.append(y)
            ws.append(f_c[:, hd:hd + 1] * xk)
        wt = jnp.concatenate(ws, axis=1).T
        for k in range(SSD_HEADS_PER_GROUP):
            hd = g * SSD_HEADS_PER_GROUP + k
            upd = jnp.dot(wt[k * hd_dim:(k + 1) * hd_dim, :].astype(BF16), bg, preferred_element_type=F32)
            state_scr[hd] = state_scr[hd] * e_c[q - 1:q, hd:hd + 1] + upd
        yg = jnp.concatenate(ys, axis=1) * _silu(z_scr[:, g * gw:(g + 1) * gw])
        yg = yg * lax.rsqrt(jnp.mean(yg * yg, axis=-1, keepdims=True) + NORM_EPS)
        yn_scr[:, g * gw:(g + 1) * gw] = (yg * gn_ref[:, g * gw:(g + 1) * gw]).astype(BF16)

    o_ref[...] = x + jnp.dot(yn_scr[...], wout_ref[...], preferred_element_type=F32)


def _ssd_call(x, bsz, seq, g_mix, in_proj, conv_w, conv_b, dt_bias, a_log, d, gate_norm, out_proj):
    t, dm = x.shape
    q = math.gcd(seq, SSD_CHUNK_MAX)
    nc = seq // q
    conv_dim = SSD_INNER + 2 * SSD_GROUPS * SSD_STATE
    wz = in_proj[:, :SSD_INNER].astype(BF16)
    wxbc = in_proj[:, SSD_INNER:SSD_INNER + conv_dim].astype(BF16)
    wdt = in_proj[:, SSD_INNER + conv_dim:]
    wdt_c = jnp.pad(wdt, ((0, 0), (0, LANES - SSD_HEADS))).astype(BF16)
    wdt_r = wdt.T.astype(BF16)
    a = -jnp.exp(a_log.astype(F32))
    lane_pad = (0, LANES - SSD_HEADS)
    args = (
        x, g_mix.reshape(1, dm), wz, wxbc, wdt_c, wdt_r, conv_w, conv_b.reshape(1, conv_dim),
        jnp.pad(dt_bias, lane_pad).reshape(1, LANES), dt_bias.reshape(SSD_HEADS, 1),
        jnp.pad(a, lane_pad).reshape(1, LANES), a.reshape(SSD_HEADS, 1),
        jnp.repeat(d, SSD_HEAD_DIM).reshape(1, SSD_INNER), gate_norm.reshape(1, SSD_INNER),
        out_proj.astype(BF16),
    )
    in_specs = [pl.BlockSpec((q, dm), lambda b, c: (b * nc + c, 0))] + [_resident(a_.shape) for a_ in args[1:]]
    return pl.pallas_call(
        functools.partial(_ssd_kernel, q=q),
        grid=(bsz, nc),
        in_specs=in_specs,
        out_specs=pl.BlockSpec((q, dm), lambda b, c: (b * nc + c, 0)),
        out_shape=jax.ShapeDtypeStruct((t, dm), F32),
        scratch_shapes=[
            pltpu.VMEM((q + 2 * SUBLANES, conv_dim), F32),
            pltpu.VMEM((q, SSD_INNER), F32),
            pltpu.VMEM((q, SSD_GROUPS * SSD_STATE), F32),
            pltpu.VMEM((q, SSD_GROUPS * SSD_STATE), F32),
            pltpu.VMEM((q, SSD_INNER), F32),
            pltpu.VMEM((q, SSD_INNER), BF16),
            pltpu.VMEM((SSD_HEADS, SSD_HEAD_DIM, SSD_STATE), F32),
        ],
        compiler_params=_params("arbitrary", "arbitrary"),
        name="ssd_mixer",
    )(*args)


def _norm_kernel(x_ref, g_ref, o_ref):
    o_ref[...] = _rms(x_ref[...], g_ref[...]).astype(o_ref.dtype)


def _norm_call(x, g, dtype):
    t, d = x.shape
    tm = 1024
    return pl.pallas_call(
        _norm_kernel,
        grid=(t // tm,),
        in_specs=[pl.BlockSpec((tm, d), lambda i: (i, 0)), pl.BlockSpec((1, d), lambda i: (0, 0))],
        out_specs=pl.BlockSpec((tm, d), lambda i: (i, 0)),
        out_shape=jax.ShapeDtypeStruct((t, d), dtype),
        compiler_params=_params("parallel"),
        name="rmsnorm",
    )(x, g.reshape(1, d))


def _s5_scan_kernel(u_ref, tm_ref, gm_ref, cm_ref, lam_ref, y_ref, g_scr, hp_scr, *, bsz, n_sub):
    half = u_ref.shape[1] // 2
    u = u_ref[...]
    g_scr[...] = jnp.dot(u, gm_ref[...], preferred_element_type=F32)
    lam = lam_ref[...]
    lr = jnp.broadcast_to(lam[0:1, :], (bsz, LANES))
    li = jnp.broadcast_to(lam[1:2, :], (bsz, LANES))

    def step(j, carry):
        hr, hi = carry
        rows = pl.ds(pl.multiple_of(j * bsz, bsz), bsz)
        hp_scr[rows, :] = jnp.concatenate([hr, hi], axis=1).astype(BF16)
        gj = g_scr[rows, :]
        nr = lr * hr - li * hi + gj[:, :LANES]
        ni = lr * hi + li * hr + gj[:, LANES:]
        return nr, ni

    zero = jnp.zeros((bsz, LANES), F32)
    lax.fori_loop(0, n_sub, step, (zero, zero))
    y_inter = jnp.dot(hp_scr[...], cm_ref[...], preferred_element_type=F32)
    y_a = jnp.dot(u[:, :half], tm_ref[0], preferred_element_type=F32)
    y_b = jnp.dot(u[:, half:], tm_ref[1], preferred_element_type=F32)
    y_ref[...] = y_inter + jnp.concatenate([y_a, y_b], axis=1)


def _s5_operators(b_re, b_im, c_re, c_im, lam_re, lam_im, log_step):
    hp = lax.Precision.HIGHEST
    s = S5_SUB
    n_g, n_p = lam_re.shape
    lam = lax.complex(lam_re.astype(F32), lam_im.astype(F32))
    step = jnp.exp(log_step.astype(F32))[:, None]
    tau = jnp.arange(s + 1, dtype=F32)[:, None, None]
    lam_pow = jnp.exp((lam * step)[None] * tau)
    b_bar = ((lam_pow[1] - 1.0) / lam)[..., None] * lax.complex(b_re.astype(F32), b_im.astype(F32))
    cc = lax.complex(c_re.astype(F32), c_im.astype(F32))
    kern = jnp.einsum('gcp,tgp,gpd->tgcd', cc, lam_pow[:s], b_bar, precision=hp).real
    ti = jnp.arange(s)
    lag = ti[None, :] - ti[:, None]
    tmat = jnp.where((lag >= 0)[:, :, None, None, None], kern[jnp.clip(lag, 0, s - 1)], 0.0)
    tmat = tmat.transpose(2, 0, 4, 1, 3).reshape(n_g, s * S5_GROUP_WIDTH, s * S5_GROUP_WIDTH)
    gm = lam_pow[s - 1 - ti][:, :, :, None] * b_bar[None]
    gm = gm.transpose(1, 0, 3, 2).reshape(n_g, s * S5_GROUP_WIDTH, n_p)
    em = cc[None] * lam_pow[1:s + 1][:, :, None, :]
    em = em.transpose(1, 3, 0, 2).reshape(n_g, n_p, s * S5_GROUP_WIDTH)
    n_pair = n_g // 2
    width = s * S5_GROUP_WIDTH
    gm_p = jnp.zeros((n_pair, 2, width, 2, 2, n_p), F32)
    cm_p = jnp.zeros((n_pair, 2, 2, n_p, 2, width), F32)
    gm2 = gm.reshape(n_pair, 2, width, n_p)
    em2 = em.reshape(n_pair, 2, n_p, width)
    for gi in range(2):
        gm_p = gm_p.at[:, gi, :, 0, gi, :].set(gm2[:, gi].real)
        gm_p = gm_p.at[:, gi, :, 1, gi, :].set(gm2[:, gi].imag)
        cm_p = cm_p.at[:, 0, gi, :, gi, :].set(em2[:, gi].real)
        cm_p = cm_p.at[:, 1, gi, :, gi, :].set(-em2[:, gi].imag)
    gm_p = gm_p.reshape(n_pair, 2 * width, 4 * n_p)
    cm_p = cm_p.reshape(n_pair, 4 * n_p, 2 * width)
    lam_s = lam_pow[s].reshape(n_pair, 2 * n_p)
    lam_p = jnp.stack([lam_s.real, lam_s.imag], axis=1)
    return tmat.astype(BF16), gm_p.astype(BF16), cm_p.astype(BF16), lam_p


def _glu_kernel(x_ref, y_ref, g_ref, d_ref, w_ref, b_ref, o_ref):
    x = x_ref[...]
    d_model = x.shape[1]
    h = _rms(x, g_ref[...])
    act = jax.nn.gelu(y_ref[...] + d_ref[...] * h).astype(BF16)
    vg = jnp.dot(act, w_ref[...], preferred_element_type=F32) + b_ref[...]
    o_ref[...] = x + vg[:, :d_model] * jax.nn.sigmoid(vg[:, d_model:])


def _s5_call(x, bsz, seq, g_mix, b_re, b_im, c_re, c_im, lam_re, lam_im, log_step, d, glu_w, glu_b):
    t, dm = x.shape
    s = S5_SUB
    n_g = dm // S5_GROUP_WIDTH
    n_pair = n_g // 2
    n_sub = seq // s
    width = s * S5_GROUP_WIDTH
    assert bsz % SUBLANES == 0 and seq % s == 0 and 2 * S5_STATE == LANES
    tmat, gm_p, cm_p, lam_p = _s5_operators(b_re, b_im, c_re, c_im, lam_re, lam_im, log_step)
    h = _norm_call(x, g_mix, BF16)
    u = h.reshape(bsz, n_sub, s, n_pair, 2, S5_GROUP_WIDTH).transpose(3, 1, 0, 4, 2, 5)
    rows = n_sub * bsz
    u = u.reshape(n_pair, rows, 2 * width)
    y = pl.pallas_call(
        functools.partial(_s5_scan_kernel, bsz=bsz, n_sub=n_sub),
        grid=(n_pair,),
        in_specs=[
            pl.BlockSpec((None, rows, 2 * width), lambda p: (p, 0, 0)),
            pl.BlockSpec((2, width, width), lambda p: (p, 0, 0)),
            pl.BlockSpec((None, 2 * width, 2 * LANES), lambda p: (p, 0, 0)),
            pl.BlockSpec((None, 2 * LANES, 2 * width), lambda p: (p, 0, 0)),
            pl.BlockSpec((None, 2, LANES), lambda p: (p, 0, 0)),
        ],
        out_specs=pl.BlockSpec((None, rows, 2 * width), lambda p: (p, 0, 0)),
        out_shape=jax.ShapeDtypeStruct((n_pair, rows, 2 * width), F32),
        scratch_shapes=[pltpu.VMEM((rows, 2 * LANES), F32), pltpu.VMEM((rows, 2 * LANES), BF16)],
        compiler_params=_params("parallel"),
        name="s5_scan",
    )(u, tmat, gm_p, cm_p, lam_p)
    y = y.reshape(n_pair, n_sub, bsz, 2, s, S5_GROUP_WIDTH).transpose(2, 1, 4, 0, 3, 5).reshape(t, dm)
    tm = 512
    return pl.pallas_call(
        _glu_kernel,
        grid=(t // tm,),
        in_specs=[
            pl.BlockSpec((tm, dm), lambda i: (i, 0)),
            pl.BlockSpec((tm, dm), lambda i: (i, 0)),
            _resident((1, dm)), _resident((1, dm)), _resident((dm, 2 * dm)), _resident((1, 2 * dm)),
        ],
        out_specs=pl.BlockSpec((tm, dm), lambda i: (i, 0)),
        out_shape=jax.ShapeDtypeStruct((t, dm), F32),
        compiler_params=_params("parallel"),
        name="s5_glu",
    )(x, y, g_mix.reshape(1, dm), d.reshape(1, dm), glu_w.astype(BF16), glu_b.reshape(1, 2 * dm))


def _rope128(tile, cos, sin_hi, sin_lo):
    return tile * cos + pltpu.roll(tile, 32, 1) * sin_hi + pltpu.roll(tile, LANES - 32, 1) * sin_lo


def _dsa_proj_kernel(x_ref, g_ref, wq_ref, wqi_ref, wsm_ref, tab_ref, tabh_ref,
                     q_ref, qi_ref, k_ref, v_ref, ki_ref, wi_ref):
    h = _rms(x_ref[...], g_ref[...]).astype(BF16)
    cos, s_hi, s_lo = tab_ref[0], tab_ref[1], tab_ref[2]
    q = jnp.dot(h, wq_ref[...], preferred_element_type=F32)
    scale = ATTN_HEAD_DIM ** -0.5
    for j in range(q.shape[1] // LANES):
        sl = slice(j * LANES, (j + 1) * LANES)
        q_ref[:, sl] = (_rope128(q[:, sl], cos, s_hi, s_lo) * scale).astype(BF16)
    qi = jnp.dot(h, wqi_ref[...], preferred_element_type=F32)
    for j in range(qi.shape[1] // LANES):
        sl = slice(j * LANES, (j + 1) * LANES)
        qi_ref[:, sl] = _rope128(qi[:, sl], cos, s_hi, s_lo).astype(BF16)
    sm = jnp.dot(h, wsm_ref[...], preferred_element_type=F32)
    kv = _rope128(sm[:, :LANES], tabh_ref[0], tabh_ref[1], tabh_ref[2])
    kw = _rope128(sm[:, LANES:], tabh_ref[3], tabh_ref[1], tabh_ref[2])
    k_ref[...] = kv[:, :ATTN_HEAD_DIM].astype(BF16)
    v_ref[...] = kv[:, ATTN_HEAD_DIM:].astype(BF16)
    ki_ref[...] = kw[:, :IDX_DIM].astype(BF16)
    wi_ref[...] = kw


def _dsa_attn_kernel(q_ref, qi_ref, wi_ref, k_ref, v_ref, ki_ref, o_ref, key_scr, bias_scr, *, top_k):
    n = pl.program_id(1)
    qb, seq = key_scr.shape
    kib = ki_ref[...]
    wi = wi_ref[...]
    idx = jnp.zeros((qb, seq), F32)
    for hh in range(IDX_HEADS):
        s = lax.dot_general(qi_ref[:, hh * IDX_DIM:(hh + 1) * IDX_DIM], kib, _NT, preferred_element_type=F32)
        idx = idx + jnp.maximum(s, 0.0) * wi[:, IDX_DIM + hh:IDX_DIM + hh + 1]
    t_pos = n * qb + lax.broadcasted_iota(jnp.int32, (qb, 1), 0)
    s_pos = lax.broadcasted_iota(jnp.int32, (1, seq), 1)
    causal = s_pos <= t_pos
    idx = jnp.where(causal, idx, -jnp.inf)
    bits = lax.bitcast_convert_type(idx, jnp.int32)
    key_scr[...] = jnp.where(bits < 0, bits ^ jnp.int32(0x7FFFFFFF), bits)

    def count(pred):
        return jnp.sum(jnp.where(pred, 1.0, 0.0), axis=1, keepdims=True)

    int_min = jnp.int32(-2 ** 31)
    kf = float(top_k)
    thr0 = jnp.where(count(key_scr[...] >= 0) >= kf, jnp.int32(0), int_min)

    def thr_step(i, thr):
        cand = thr + lax.shift_left(jnp.int32(1), jnp.int32(30) - i)
        return jnp.where(count(key_scr[...] >= cand) >= kf, cand, thr)

    thr = lax.fori_loop(0, 31, thr_step, thr0)
    keys = key_scr[...]
    need = kf - count(keys > thr)

    def cut_step(i, cut):
        cand = cut + lax.shift_left(jnp.int32(1), jnp.int32(seq.bit_length() - 2) - i)
        return jnp.where(count((key_scr[...] == thr) & (s_pos < cand)) < need, cand, cut)

    cut = lax.fori_loop(0, seq.bit_length() - 1, cut_step, jnp.zeros((qb, 1), jnp.int32))
    sel = (keys > thr) | ((keys == thr) & (s_pos <= cut))
    bias_scr[...] = jnp.where(sel & causal, 0.0, -jnp.inf)

    kb = k_ref[...]
    vb = v_ref[...]
    for hp in range(ATTN_HEADS // 2):
        outs = []
        for hh in (2 * hp, 2 * hp + 1):
            qh = q_ref[:, hh * ATTN_HEAD_DIM:(hh + 1) * ATTN_HEAD_DIM]
            lg = lax.dot_general(qh, kb, _NT, preferred_element_type=F32) + bias_scr[...]
            p = jnp.exp(lg - jnp.max(lg, axis=1, keepdims=True))
            den = jnp.sum(p, axis=1, keepdims=True)
            outs.append(jnp.dot(p.astype(BF16), vb, preferred_element_type=F32) / den)
        o_ref[:, hp * LANES:(hp + 1) * LANES] = jnp.concatenate(outs, axis=1).astype(BF16)


def _proj_res_kernel(a_ref, w_ref, x_ref, o_ref):
    o_ref[...] = x_ref[...] + jnp.dot(a_ref[...], w_ref[...], preferred_element_type=F32)


def _rope_tables(seq):
    half = ATTN_HEAD_DIM // 2
    inv = ROPE_THETA ** (-jnp.arange(half, dtype=F32) / half)
    ang = jnp.arange(seq, dtype=F32)[:, None] * inv
    cos, sin = jnp.cos(ang), jnp.sin(ang)
    zero, one = jnp.zeros_like(cos), jnp.ones_like(cos)
    cos64 = jnp.concatenate([cos, cos], axis=1)
    hi64 = jnp.concatenate([zero, sin], axis=1)
    lo64 = jnp.concatenate([-sin, zero], axis=1)
    full = jnp.stack([jnp.tile(t, (1, 2)) for t in (cos64, hi64, lo64)])
    ident = jnp.concatenate([one, one], axis=1)
    zero64 = jnp.concatenate([zero, zero], axis=1)
    wscale = ident * (IDX_HEADS ** -0.5 * IDX_DIM ** -0.5)
    halfs = jnp.stack([jnp.concatenate([cos64, ident], axis=1), jnp.concatenate([hi64, zero64], axis=1),
                       jnp.concatenate([lo64, zero64], axis=1), jnp.concatenate([cos64, wscale], axis=1)])
    return full, halfs


def _dsa_call(x, bsz, seq, g_mix, w_in, w_out):
    t, dm = x.shape
    dq = ATTN_HEADS * ATTN_HEAD_DIM
    dqi = IDX_HEADS * IDX_DIM
    o_k, o_v, o_qi = dq, dq + ATTN_HEAD_DIM, dq + 2 * ATTN_HEAD_DIM
    o_ki = o_qi + dqi
    o_wi = o_ki + IDX_DIM
    wq = w_in[:, :o_k].astype(BF16)
    wqi = w_in[:, o_qi:o_ki].astype(BF16)
    wsm = jnp.concatenate([w_in[:, o_k:o_qi], w_in[:, o_ki:],
                           jnp.zeros((dm, LANES - IDX_DIM - IDX_HEADS), w_in.dtype)], axis=1).astype(BF16)
    full, halfs = _rope_tables(seq)
    tm = 256
    nt = seq // tm
    row = lambda i: (i, 0)
    tab = lambda i: (0, i % nt, 0)
    q, qi, k, v, ki, wi = pl.pallas_call(
        _dsa_proj_kernel,
        grid=(t // tm,),
        in_specs=[
            pl.BlockSpec((tm, dm), row), _resident((1, dm)), _resident(wq.shape), _resident(wqi.shape),
            _resident(wsm.shape),
            pl.BlockSpec((3, tm, LANES), tab), pl.BlockSpec((4, tm, LANES), tab),
        ],
        out_specs=[
            pl.BlockSpec((tm, dq), row), pl.BlockSpec((tm, dqi), row),
            pl.BlockSpec((tm, ATTN_HEAD_DIM), row), pl.BlockSpec((tm, ATTN_HEAD_DIM), row),
            pl.BlockSpec((tm, IDX_DIM), row), pl.BlockSpec((tm, LANES), row),
        ],
        out_shape=[
            jax.ShapeDtypeStruct((t, dq), BF16), jax.ShapeDtypeStruct((t, dqi), BF16),
            jax.ShapeDtypeStruct((t, ATTN_HEAD_DIM), BF16), jax.ShapeDtypeStruct((t, ATTN_HEAD_DIM), BF16),
            jax.ShapeDtypeStruct((t, IDX_DIM), BF16), jax.ShapeDtypeStruct((t, LANES), F32),
        ],
        compiler_params=_params("parallel"),
        name="dsa_proj",
    )(x, g_mix.reshape(1, dm), wq, wqi, wsm, full, halfs)

    qb = QUERY_BLOCK
    nb = seq // qb
    top_k = min(TOPK_MAX, seq // 4)
    blk = lambda b, n: (b * nb + n, 0)
    per_b = lambda b, n: (b, 0, 0)
    o = pl.pallas_call(
        functools.partial(_dsa_attn_kernel, top_k=top_k),
        grid=(bsz, nb),
        in_specs=[
            pl.BlockSpec((qb, dq), blk), pl.BlockSpec((qb, dqi), blk), pl.BlockSpec((qb, LANES), blk),
            pl.BlockSpec((None, seq, ATTN_HEAD_DIM), per_b), pl.BlockSpec((None, seq, ATTN_HEAD_DIM), per_b),
            pl.BlockSpec((None, seq, IDX_DIM), per_b),
        ],
        out_specs=pl.BlockSpec((qb, dq), blk),
        out_shape=jax.ShapeDtypeStruct((t, dq), BF16),
        scratch_shapes=[pltpu.VMEM((qb, seq), jnp.int32), pltpu.VMEM((qb, seq), F32)],
        compiler_params=_params("parallel", "arbitrary"),
        name="dsa_attn",
    )(q, qi, wi, k.reshape(bsz, seq, -1), v.reshape(bsz, seq, -1), ki.reshape(bsz, seq, -1))

    tmo = 512
    return pl.pallas_call(
        _proj_res_kernel,
        grid=(t // tmo,),
        in_specs=[pl.BlockSpec((tmo, dq), row), _resident((dq, dm)), pl.BlockSpec((tmo, dm), row)],
        out_specs=pl.BlockSpec((tmo, dm), row),
        out_shape=jax.ShapeDtypeStruct((t, dm), F32),
        compiler_params=_params("parallel"),
        name="dsa_out",
    )(o, w_out.astype(BF16), x)


def kernel(x, ffn1_norm, ffn1_w_in, ffn1_w_out, mix_norm, ffn2_norm, ffn2_w_in, ffn2_w_out, ssd_in_proj, ssd_conv_w, ssd_conv_b, ssd_dt_bias, ssd_a_log, ssd_d, ssd_gate_norm, ssd_out_proj, s5_b_re, s5_b_im, s5_c_re, s5_c_im, s5_lam_re, s5_lam_im, s5_log_step, s5_d, s5_glu_w, s5_glu_b, dsa_in_proj, dsa_out_proj, final_norm):
    bsz, seq, dm = x.shape
    depth = ffn1_norm.shape[0]
    xt = x.reshape(bsz * seq, dm)
    for i in range(depth):
        j, kind = divmod(i, N_MIXERS)
        xt = _ffn_call(xt, ffn1_norm[i], ffn1_w_in[i].astype(BF16), ffn1_w_out[i].astype(BF16))
        if kind == 0:
            xt = _ssd_call(xt, bsz, seq, mix_norm[i], ssd_in_proj[j], ssd_conv_w[j], ssd_conv_b[j],
                           ssd_dt_bias[j], ssd_a_log[j], ssd_d[j], ssd_gate_norm[j], ssd_out_proj[j])
        elif kind == 1:
            xt = _s5_call(xt, bsz, seq, mix_norm[i], s5_b_re[j], s5_b_im[j], s5_c_re[j], s5_c_im[j],
                          s5_lam_re[j], s5_lam_im[j], s5_log_step[j], s5_d[j], s5_glu_w[j], s5_glu_b[j])
        else:
            xt = _dsa_call(xt, bsz, seq, mix_norm[i], dsa_in_proj[j], dsa_out_proj[j])
        post = final_norm if i == depth - 1 else None
        xt = _ffn_call(xt, ffn2_norm[i], ffn2_w_in[i].astype(BF16), ffn2_w_out[i].astype(BF16), post)
    return xt.reshape(bsz, seq, dm)
```

```python
import functools
import math

import jax
import jax.numpy as jnp
from jax import lax
from jax.experimental import pallas as pl
from jax.experimental.pallas import tpu as pltpu

F32 = jnp.float32
BF16 = jnp.bfloat16

NORM_EPS = 1e-6
N_MIXERS = 3

SSD_HEAD_DIM = 64
SSD_GROUPS = 8
SSD_HEADS_PER_GROUP = 4
SSD_HEADS = SSD_GROUPS * SSD_HEADS_PER_GROUP
SSD_STATE = 128
SSD_INNER = SSD_HEADS * SSD_HEAD_DIM
SSD_CONV = 4
SSD_CHUNK_MAX = 256
SSD_GROUP_WIDTH = SSD_HEADS_PER_GROUP * SSD_HEAD_DIM

S5_GROUP_WIDTH = 16
S5_STATE = 64
S5_SUB = 16

ATTN_HEADS = 16
ATTN_HEAD_DIM = 64
IDX_HEADS = 8
IDX_DIM = 64
TOPK_MAX = 256
QUERY_BLOCK = 128
ROPE_THETA = 10000.0

LANES = 128
SUBLANES = 8
VMEM_LIMIT_BYTES = 56 * 1024 * 1024

_NT = (((1,), (1,)), ((), ()))


def _params(*semantics):
    return pltpu.CompilerParams(dimension_semantics=semantics, vmem_limit_bytes=VMEM_LIMIT_BYTES)


def _rms(x, g):
    ms = jnp.mean(x * x, axis=-1, keepdims=True)
    return x * lax.rsqrt(ms + NORM_EPS) * g


def _silu(x):
    return x * jax.nn.sigmoid(x)


def _softplus(x):
    return jnp.maximum(x, 0.0) + jnp.log1p(jnp.exp(-jnp.abs(x)))


def _resident(shape):
    zeros = (0,) * len(shape)
    return pl.BlockSpec(shape, lambda *_: zeros, pipeline_mode=pl.Buffered(1))


def _ffn_kernel(x_ref, g_ref, wg_ref, wu_ref, wo_ref, pg_ref, o_ref, h_scr, acc_scr, *, post_norm):
    j = pl.program_id(1)

    @pl.when(j == 0)
    def _():
        h_scr[...] = _rms(x_ref[...], g_ref[...]).astype(BF16)
        acc_scr[...] = jnp.zeros_like(acc_scr)

    h = h_scr[...]
    gate = jnp.dot(h, wg_ref[...], preferred_element_type=F32)
    up = jnp.dot(h, wu_ref[...], preferred_element_type=F32)
    act = (_silu(gate) * up).astype(BF16)
    acc_scr[...] += jnp.dot(act, wo_ref[...], preferred_element_type=F32)

    @pl.when(j == pl.num_programs(1) - 1)
    def _():
        y = x_ref[...] + 0.5 * acc_scr[...]
        if post_norm:
            y = _rms(y, pg_ref[...])
        o_ref[...] = y


def _ffn_call(x, g, w_in, w_out, post_g=None):
    t, d = x.shape
    f = w_out.shape[0]
    tm = 512
    nj = 2
    tf = f // nj
    assert t % tm == 0 and f % nj == 0 and tf % LANES == 0
    post_norm = post_g is not None
    pg = post_g if post_norm else g
    return pl.pallas_call(
        functools.partial(_ffn_kernel, post_norm=post_norm),
        grid=(t // tm, nj),
        in_specs=[
            pl.BlockSpec((tm, d), lambda i, j: (i, 0)),
            pl.BlockSpec((1, d), lambda i, j: (0, 0)),
            pl.BlockSpec((d, tf), lambda i, j: (0, j)),
            pl.BlockSpec((d, tf), lambda i, j: (0, j + nj)),
            pl.BlockSpec((tf, d), lambda i, j: (j, 0)),
            pl.BlockSpec((1, d), lambda i, j: (0, 0)),
        ],
        out_specs=pl.BlockSpec((tm, d), lambda i, j: (i, 0)),
        out_shape=jax.ShapeDtypeStruct((t, d), F32),
        scratch_shapes=[pltpu.VMEM((tm, d), BF16), pltpu.VMEM((tm, d), F32)],
        compiler_params=_params("parallel", "arbitrary"),
        name="ffn",
    )(x, g.reshape(1, d), w_in, w_in, w_out, pg.reshape(1, d))


def _split3(a):
    hi = a.astype(BF16)
    r1 = a - hi.astype(F32)
    mid = r1.astype(BF16)
    lo = (r1 - mid.astype(F32)).astype(BF16)
    return hi, mid, lo


def _ssd_kernel(x_ref, gmix_ref, wz_ref, wxbc_ref, wdt_ref, wdtt_ref, convw_ref, convb_ref,
                dtbc_ref, dtbr_ref, ac_ref, ar_ref, dvec_ref, gn_ref, wout_ref,
                o_ref,
                xbc_scr, xs_scr, bm_scr, cm_scr, z_scr, yn_scr, state_scr, *, q):
    c = pl.program_id(1)
    pad = SUBLANES
    gw = SSD_GROUP_WIDTH
    hd_dim = SSD_HEAD_DIM
    n_state = SSD_STATE

    @pl.when(c == 0)
    def _():
        state_scr[...] = jnp.zeros_like(state_scr)
        xbc_scr[0:pad, :] = jnp.zeros((pad, xbc_scr.shape[1]), F32)

    @pl.when(c > 0)
    def _():
        xbc_scr[0:pad, :] = xbc_scr[q:q + pad, :]

    x = x_ref[...]
    h = _rms(x, gmix_ref[...]).astype(BF16)
    xbc_scr[pad:pad + q, :] = jnp.dot(h, wxbc_ref[...], preferred_element_type=F32)
    z_scr[...] = jnp.dot(h, wz_ref[...], preferred_element_type=F32)

    cw = 512
    n_x = SSD_INNER // cw
    n_b = SSD_GROUPS * n_state // cw
    for j in range(xbc_scr.shape[1] // cw):
        sl = slice(j * cw, (j + 1) * cw)
        w = convw_ref[:, sl]
        acc = xbc_scr[pad - 3:pad - 3 + q, sl] * w[0:1]
        acc = acc + xbc_scr[pad - 2:pad - 2 + q, sl] * w[1:2]
        acc = acc + xbc_scr[pad - 1:pad - 1 + q, sl] * w[2:3]
        acc = acc + xbc_scr[pad:pad + q, sl] * w[3:4]
        act = _silu(acc + convb_ref[:, sl])
        if j < n_x:
            xs_scr[:, sl] = act
        elif j < n_x + n_b:
            bm_scr[:, (j - n_x) * cw:(j - n_x + 1) * cw] = act
        else:
            cm_scr[:, (j - n_x - n_b) * cw:(j - n_x - n_b + 1) * cw] = act

    dt_c = _softplus(jnp.dot(h, wdt_ref[...], preferred_element_type=F32) + dtbc_ref[...])
    dt_r = _softplus(lax.dot_general(wdtt_ref[...], h, _NT, preferred_element_type=F32) + dtbr_ref[...])
    a_c = dt_c * ac_ref[...]
    a_r = dt_r * ar_ref[...]
    row = lax.broadcasted_iota(jnp.int32, (q, q), 0)
    col = lax.broadcasted_iota(jnp.int32, (q, q), 1)
    causal = col <= row
    tri_low = jnp.where(causal, 1.0, 0.0).astype(BF16)
    tri_up = jnp.where(row <= col, 1.0, 0.0).astype(BF16)
    cs_c = sum(jnp.dot(tri_low, p, preferred_element_type=F32) for p in _split3(a_c))
    cs_r = sum(jnp.dot(p, tri_up, preferred_element_type=F32) for p in _split3(a_r))
    e_c = jnp.exp(cs_c)
    f_c = dt_c * jnp.exp(cs_c[q - 1:q, :] - cs_c)

    for g in range(SSD_GROUPS):
        bg = bm_scr[:, g * n_state:(g + 1) * n_state].astype(BF16)
        cg = cm_scr[:, g * n_state:(g + 1) * n_state].astype(BF16)
        cb = lax.dot_general(cg, bg, _NT, preferred_element_type=F32)
        xs_g = xs_scr[:, g * gw:(g + 1) * gw]
        ys, ws = [], []
        for k in range(SSD_HEADS_PER_GROUP):
            hd = g * SSD_HEADS_PER_GROUP + k
            seg = cs_c[:, hd:hd + 1] - cs_r[hd:hd + 1, :]
            decay = jnp.exp(jnp.where(causal, seg, -jnp.inf))
            m = (cb * decay).astype(BF16)
            xk = xs_g[:, k * hd_dim:(k + 1) * hd_dim]
            dtx = dt_c[:, hd:hd + 1] * xk
            y = jnp.dot(m, dtx.astype(BF16), preferred_element_type=F32)
            st = state_scr[hd].astype(BF16)
            y = y + lax.dot_general(cg, st, _NT, preferred_element_type=F32) * e_c[:, hd:hd + 1]
            y = y + dvec_ref[:, hd * hd_dim:(hd + 1) * hd_dim] * xk
            ys.append(y)
            ws.append(f_c[:, hd:hd + 1] * xk)
        wt = jnp.concatenate(ws, axis=1).T
        for k in range(SSD_HEADS_PER_GROUP):
            hd = g * SSD_HEADS_PER_GROUP + k
            upd = jnp.dot(wt[k * hd_dim:(k + 1) * hd_dim, :].astype(BF16), bg, preferred_element_type=F32)
            state_scr[hd] = state_scr[hd] * e_c[q - 1:q, hd:hd + 1] + upd
        yg = jnp.concatenate(ys, axis=1) * _silu(z_scr[:, g * gw:(g + 1) * gw])
        yg = yg * lax.rsqrt(jnp.mean(yg * yg, axis=-1, keepdims=True) + NORM_EPS)
        yn_scr[:, g * gw:(g + 1) * gw] = (yg * gn_ref[:, g * gw:(g + 1) * gw]).astype(BF16)

    o_ref[...] = x + jnp.dot(yn_scr[...], wout_ref[...], preferred_element_type=F32)


def _ssd_call(x, bsz, seq, g_mix, in_proj, conv_w, conv_b, dt_bias, a_log, d, gate_norm, out_proj):
    t, dm = x.shape
    q = math.gcd(seq, SSD_CHUNK_MAX)
    nc = seq // q
    conv_dim = SSD_INNER + 2 * SSD_GROUPS * SSD_STATE
    wz = in_proj[:, :SSD_INNER].astype(BF16)
    wxbc = in_proj[:, SSD_INNER:SSD_INNER + conv_dim].astype(BF16)
    wdt = in_proj[:, SSD_INNER + conv_dim:]
    wdt_c = jnp.pad(wdt, ((0, 0), (0, LANES - SSD_HEADS))).astype(BF16)
    wdt_r = wdt.T.astype(BF16)
    a = -jnp.exp(a_log.astype(F32))
    lane_pad = (0, LANES - SSD_HEADS)
    args = (
        x, g_mix.reshape(1, dm), wz, wxbc, wdt_c, wdt_r, conv_w, conv_b.reshape(1, conv_dim),
        jnp.pad(dt_bias, lane_pad).reshape(1, LANES), dt_bias.reshape(SSD_HEADS, 1),
        jnp.pad(a, lane_pad).reshape(1, LANES), a.reshape(SSD_HEADS, 1),
        jnp.repeat(d, SSD_HEAD_DIM).reshape(1, SSD_INNER), gate_norm.reshape(1, SSD_INNER),
        out_proj.astype(BF16),
    )
    in_specs = [pl.BlockSpec((q, dm), lambda b, c: (b * nc + c, 0))] + [_resident(a_.shape) for a_ in args[1:]]
    return pl.pallas_call(
        functools.partial(_ssd_kernel, q=q),
        grid=(bsz, nc),
        in_specs=in_specs,
        out_specs=pl.BlockSpec((q, dm), lambda b, c: (b * nc + c, 0)),
        out_shape=jax.ShapeDtypeStruct((t, dm), F32),
        scratch_shapes=[
            pltpu.VMEM((q + 2 * SUBLANES, conv_dim), F32),
            pltpu.VMEM((q, SSD_INNER), F32),
            pltpu.VMEM((q, SSD_GROUPS * SSD_STATE), F32),
            pltpu.VMEM((q, SSD_GROUPS * SSD_STATE), F32),
            pltpu.VMEM((q, SSD_INNER), F32),
            pltpu.VMEM((q, SSD_INNER), BF16),
            pltpu.VMEM((SSD_HEADS, SSD_HEAD_DIM, SSD_STATE), F32),
        ],
        compiler_params=_params("arbitrary", "arbitrary"),
        name="ssd_mixer",
    )(*args)


S5_GB = LANES // S5_GROUP_WIDTH


def _s5_pack_kernel(x_ref, g_ref, o_ref, h_scr):
    h = _rms(x_ref[...], g_ref[...])
    rows = o_ref.shape[1]
    for gb in range(o_ref.shape[0]):
        h_scr[gb] = h[:, gb * LANES:(gb + 1) * LANES]
    for s in range(S5_SUB):
        for gb in range(o_ref.shape[0]):
            piece = h_scr[gb, pl.ds(s, rows, stride=S5_SUB), :]
            o_ref[gb, :, s * LANES:(s + 1) * LANES] = piece.astype(o_ref.dtype)


def _s5_scan_kernel(u_ref, mop_ref, gop_ref, cop_ref, lam_ref, y_ref, g_scr, hp_scr, *, n_bt, n_sub):
    n_half = g_scr.shape[0] // 2
    tile = 2 * LANES
    u = u_ref[...]
    g_in = jnp.dot(u, gop_ref[...], preferred_element_type=F32)
    for c in range(2 * n_half):
        g_scr[c] = g_in[:, c * LANES:(c + 1) * LANES]
    lam = lam_ref[...]
    lr = [jnp.broadcast_to(lam[0:1, c * LANES:(c + 1) * LANES], (n_bt, LANES)) for c in range(n_half)]
    li = [jnp.broadcast_to(lam[1:2, c * LANES:(c + 1) * LANES], (n_bt, LANES)) for c in range(n_half)]
    hr = [jnp.zeros((n_bt, LANES), F32)] * n_half
    hi = [jnp.zeros((n_bt, LANES), F32)] * n_half
    for j in range(n_sub):
        rows = pl.ds(j, n_bt, stride=n_sub)
        for c in range(n_half):
            hp_scr[c, rows, :] = hr[c]
            hp_scr[n_half + c, rows, :] = hi[c]
            gr, gi = g_scr[c, rows, :], g_scr[n_half + c, rows, :]
            hr[c], hi[c] = lr[c] * hr[c] - li[c] * hi[c] + gr, lr[c] * hi[c] + li[c] * hr[c] + gi
    h_in = jnp.concatenate([hp_scr[c] for c in range(2 * n_half)], axis=1).astype(BF16)
    y_inter = jnp.dot(h_in, cop_ref[...], preferred_element_type=F32)
    for tp in range(u.shape[1] // tile):
        k_hi = (tp + 1) * tile
        y_intra = jnp.dot(u[:, :k_hi], mop_ref[0:k_hi, tp * tile:(tp + 1) * tile], preferred_element_type=F32)
        y_ref[:, tp * tile:(tp + 1) * tile] = y_inter[:, tp * tile:(tp + 1) * tile] + y_intra


def _s5_operators(b_re, b_im, c_re, c_im, lam_re, lam_im, log_step):
    hp = lax.Precision.HIGHEST
    s, gb, w = S5_SUB, S5_GB, S5_GROUP_WIDTH
    n_g, n_p = lam_re.shape
    n_blk = n_g // gb
    lam = lax.complex(lam_re.astype(F32), lam_im.astype(F32))
    step = jnp.exp(log_step.astype(F32))[:, None]
    tau = jnp.arange(s + 1, dtype=F32)[:, None, None]
    lam_pow = jnp.exp((lam * step)[None] * tau)
    b_bar = ((lam_pow[1] - 1.0) / lam)[..., None] * lax.complex(b_re.astype(F32), b_im.astype(F32))
    cc = lax.complex(c_re.astype(F32), c_im.astype(F32))
    eye = jnp.eye(gb, dtype=F32)
    kern = jnp.einsum('gcp,tgp,gpd->tgcd', cc, lam_pow[:s], b_bar, precision=hp).real
    ti = jnp.arange(s)
    lag = ti[None, :] - ti[:, None]
    tm = jnp.where((lag >= 0)[:, :, None, None, None], kern[jnp.clip(lag, 0, s - 1)], 0.0)
    tm = tm.reshape(s, s, n_blk, gb, w, w)
    mop = jnp.einsum('stagcd,gh->asgdthc', tm, eye).reshape(n_blk, s * gb * w, s * gb * w)
    gm = (lam_pow[s - 1 - ti][:, :, :, None] * b_bar[None]).reshape(s, n_blk, gb, n_p, w)
    gop = jnp.stack([jnp.einsum('sagpd,gh->asgdhp', part, eye) for part in (gm.real, gm.imag)], axis=4)
    gop = gop.reshape(n_blk, s * gb * w, 2 * gb * n_p)
    em = (cc[None] * lam_pow[1:s + 1][:, :, None, :]).reshape(s, n_blk, gb, w, n_p)
    cop = jnp.stack([jnp.einsum('tagcp,gh->agpthc', part, eye) for part in (em.real, -em.imag)], axis=1)
    cop = cop.reshape(n_blk, 2 * gb * n_p, s * gb * w)
    lam_s = lam_pow[s].reshape(n_blk, gb * n_p)
    lam_blk = jnp.stack([lam_s.real, lam_s.imag], axis=1)
    return mop.astype(BF16), gop.astype(BF16), cop.astype(BF16), lam_blk


def _glu_kernel(x_ref, y_ref, g_ref, d_ref, w_ref, b_ref, o_ref, y_scr):
    x = x_ref[...]
    d_model = x.shape[1]
    rows = y_ref.shape[1]
    for s in range(S5_SUB):
        for gb in range(y_ref.shape[0]):
            y_scr[gb, pl.ds(s, rows, stride=S5_SUB), :] = y_ref[gb, :, s * LANES:(s + 1) * LANES]
    y = jnp.concatenate([y_scr[gb] for gb in range(y_ref.shape[0])], axis=1)
    h = _rms(x, g_ref[...])
    act = jax.nn.gelu(y + d_ref[...] * h).astype(BF16)
    vg = jnp.dot(act, w_ref[...], preferred_element_type=F32) + b_ref[...]
    o_ref[...] = x + vg[:, :d_model] * jax.nn.sigmoid(vg[:, d_model:])


def _s5_call(x, bsz, seq, g_mix, b_re, b_im, c_re, c_im, lam_re, lam_im, log_step, d, glu_w, glu_b):
    t, dm = x.shape
    s = S5_SUB
    n_blk = dm // LANES
    n_sub = seq // s
    rows = t // s
    width = s * LANES
    state_w = 2 * S5_GB * S5_STATE
    n_bt = 4
    tm = 512
    assert seq % s == 0 and bsz % n_bt == 0 and t % tm == 0
    mop, gop, cop, lam_blk = _s5_operators(b_re, b_im, c_re, c_im, lam_re, lam_im, log_step)
    tok = lambda i: (i, 0)
    packed = lambda i: (0, i, 0)
    u = pl.pallas_call(
        _s5_pack_kernel,
        grid=(t // tm,),
        in_specs=[pl.BlockSpec((tm, dm), tok), _resident((1, dm))],
        out_specs=pl.BlockSpec((n_blk, tm // s, width), packed),
        out_shape=jax.ShapeDtypeStruct((n_blk, rows, width), BF16),
        scratch_shapes=[pltpu.VMEM((n_blk, tm, LANES), F32)],
        compiler_params=_params("parallel"),
        name="s5_pack",
    )(x, g_mix.reshape(1, dm))
    rt = n_bt * n_sub
    per_blk = lambda a, r: (a, 0, 0)
    once = pl.Buffered(1)
    y = pl.pallas_call(
        functools.partial(_s5_scan_kernel, n_bt=n_bt, n_sub=n_sub),
        grid=(n_blk, rows // rt),
        in_specs=[
            pl.BlockSpec((None, rt, width), lambda a, r: (a, r, 0)),
            pl.BlockSpec((None, width, width), per_blk, pipeline_mode=once),
            pl.BlockSpec((None, width, state_w), per_blk, pipeline_mode=once),
            pl.BlockSpec((None, state_w, width), per_blk, pipeline_mode=once),
            pl.BlockSpec((None, 2, state_w // 2), per_blk),
        ],
        out_specs=pl.BlockSpec((None, rt, width), lambda a, r: (a, r, 0)),
        out_shape=jax.ShapeDtypeStruct((n_blk, rows, width), F32),
        scratch_shapes=[pltpu.VMEM((state_w // LANES, rt, LANES), F32)] * 2,
        compiler_params=_params("arbitrary", "arbitrary"),
        name="s5_scan",
    )(u, mop, gop, cop, lam_blk)
    return pl.pallas_call(
        _glu_kernel,
        grid=(t // tm,),
        in_specs=[
            pl.BlockSpec((tm, dm), tok),
            pl.BlockSpec((n_blk, tm // s, width), packed),
            _resident((1, dm)), _resident((1, dm)), _resident((dm, 2 * dm)), _resident((1, 2 * dm)),
        ],
        out_specs=pl.BlockSpec((tm, dm), tok),
        out_shape=jax.ShapeDtypeStruct((t, dm), F32),
        scratch_shapes=[pltpu.VMEM((n_blk, tm, LANES), F32)],
        compiler_params=_params("parallel"),
        name="s5_glu",
    )(x, y, g_mix.reshape(1, dm), d.reshape(1, dm), glu_w.astype(BF16), glu_b.reshape(1, 2 * dm))


def _rope128(tile, cos, sin_hi, sin_lo):
    return tile * cos + pltpu.roll(tile, 32, 1) * sin_hi + pltpu.roll(tile, LANES - 32, 1) * sin_lo


def _dsa_proj_kernel(x_ref, g_ref, wq_ref, wqi_ref, wsm_ref, tab_ref, tabh_ref,
                     q_ref, qi_ref, k_ref, v_ref, ki_ref, wi_ref):
    h = _rms(x_ref[...], g_ref[...]).astype(BF16)
    cos, s_hi, s_lo = tab_ref[0], tab_ref[1], tab_ref[2]
    q = jnp.dot(h, wq_ref[...], preferred_element_type=F32)
    scale = ATTN_HEAD_DIM ** -0.5
    for j in range(q.shape[1] // LANES):
        sl = slice(j * LANES, (j + 1) * LANES)
        q_ref[:, sl] = (_rope128(q[:, sl], cos, s_hi, s_lo) * scale).astype(BF16)
    qi = jnp.dot(h, wqi_ref[...], preferred_element_type=F32)
    for j in range(qi.shape[1] // LANES):
        sl = slice(j * LANES, (j + 1) * LANES)
        qi_ref[:, sl] = _rope128(qi[:, sl], cos, s_hi, s_lo).astype(BF16)
    sm = jnp.dot(h, wsm_ref[...], preferred_element_type=F32)
    kv = _rope128(sm[:, :LANES], tabh_ref[0], tabh_ref[1], tabh_ref[2])
    kw = _rope128(sm[:, LANES:], tabh_ref[3], tabh_ref[1], tabh_ref[2])
    k_ref[...] = kv[:, :ATTN_HEAD_DIM].astype(BF16)
    v_ref[...] = kv[:, ATTN_HEAD_DIM:].astype(BF16)
    ki_ref[...] = kw[:, :IDX_DIM].astype(BF16)
    wi_ref[...] = kw


def _dsa_attn_kernel(q_ref, qi_ref, wi_ref, k_ref, v_ref, ki_ref, o_ref, key_scr, bias_scr, *, top_k, n_span):
    n = pl.program_id(1)
    per_span = pl.num_programs(1) // n_span
    span_keys = key_scr.shape[1] // n_span
    for i in range(n_span):
        @pl.when(n // per_span == i)
        def _(i=i):
            _dsa_attn_block(q_ref, qi_ref, wi_ref, k_ref, v_ref, ki_ref, o_ref, key_scr, bias_scr,
                            top_k=top_k, seq=(i + 1) * span_keys)


def _dsa_attn_block(q_ref, qi_ref, wi_ref, k_ref, v_ref, ki_ref, o_ref, key_full, bias_full, *, top_k, seq):
    n = pl.program_id(1)
    qb = key_full.shape[0]
    key_scr = key_full.at[:, 0:seq]
    bias_scr = bias_full.at[:, 0:seq]
    kib = ki_ref[0:seq, :]
    wi = wi_ref[...]
    idx = jnp.zeros((qb, seq), F32)
    for hh in range(IDX_HEADS):
        s = lax.dot_general(qi_ref[:, hh * IDX_DIM:(hh + 1) * IDX_DIM], kib, _NT, preferred_element_type=F32)
        idx = idx + jnp.maximum(s, 0.0) * wi[:, IDX_DIM + hh:IDX_DIM + hh + 1]
    t_pos = n * qb + lax.broadcasted_iota(jnp.int32, (qb, 1), 0)
    s_pos = lax.broadcasted_iota(jnp.int32, (1, seq), 1)
    causal = s_pos <= t_pos
    idx = jnp.where(causal, idx, -jnp.inf)
    bits = lax.bitcast_convert_type(idx, jnp.int32)
    key_scr[...] = jnp.where(bits < 0, bits ^ jnp.int32(0x7FFFFFFF), bits)

    def count(pred):
        return jnp.sum(jnp.where(pred, 1.0, 0.0), axis=1, keepdims=True)

    int_min = jnp.int32(-2 ** 31)
    kf = float(top_k)
    thr0 = jnp.where(count(key_scr[...] >= 0) >= kf, jnp.int32(0), int_min)

    def thr_step(i, thr):
        cand = thr + lax.shift_left(jnp.int32(1), jnp.int32(30) - i)
        return jnp.where(count(key_scr[...] >= cand) >= kf, cand, thr)

    thr = lax.fori_loop(0, 31, thr_step, thr0)
    keys = key_scr[...]
    n_ge = count(keys >= thr)

    def tie_cut():
        need = kf - count(key_scr[...] > thr)
        n_bits = (seq - 1).bit_length()

        def cut_step(i, cut):
            cand = cut + lax.shift_left(jnp.int32(1), jnp.int32(n_bits - 1) - i)
            return jnp.where(count((key_scr[...] == thr) & (s_pos < cand)) < need, cand, cut)

        return lax.fori_loop(0, n_bits, cut_step, jnp.zeros((qb, 1), jnp.int32))

    cut = lax.cond(jnp.max(n_ge) > kf, tie_cut, lambda: jnp.full((qb, 1), seq, jnp.int32))
    sel = (keys > thr) | ((keys == thr) & (s_pos <= cut))
    bias_scr[...] = jnp.where(sel & causal, 0.0, -jnp.inf)

    kb = k_ref[0:seq, :]
    vb = v_ref[0:seq, :]
    for hp in range(ATTN_HEADS // 2):
        outs = []
        for hh in (2 * hp, 2 * hp + 1):
            qh = q_ref[:, hh * ATTN_HEAD_DIM:(hh + 1) * ATTN_HEAD_DIM]
            lg = lax.dot_general(qh, kb, _NT, preferred_element_type=F32) + bias_scr[...]
            p = jnp.exp(lg - jnp.max(lg, axis=1, keepdims=True))
            den = jnp.sum(p, axis=1, keepdims=True)
            outs.append(jnp.dot(p.astype(BF16), vb, preferred_element_type=F32) / den)
        o_ref[:, hp * LANES:(hp + 1) * LANES] = jnp.concatenate(outs, axis=1).astype(BF16)


def _proj_res_kernel(a_ref, w_ref, x_ref, o_ref):
    o_ref[...] = x_ref[...] + jnp.dot(a_ref[...], w_ref[...], preferred_element_type=F32)


def _rope_tables(seq):
    half = ATTN_HEAD_DIM // 2
    inv = ROPE_THETA ** (-jnp.arange(half, dtype=F32) / half)
    ang = jnp.arange(seq, dtype=F32)[:, None] * inv
    cos, sin = jnp.cos(ang), jnp.sin(ang)
    zero, one = jnp.zeros_like(cos), jnp.ones_like(cos)
    cos64 = jnp.concatenate([cos, cos], axis=1)
    hi64 = jnp.concatenate([zero, sin], axis=1)
    lo64 = jnp.concatenate([-sin, zero], axis=1)
    full = jnp.stack([jnp.tile(t, (1, 2)) for t in (cos64, hi64, lo64)])
    ident = jnp.concatenate([one, one], axis=1)
    zero64 = jnp.concatenate([zero, zero], axis=1)
    wscale = ident * (IDX_HEADS ** -0.5 * IDX_DIM ** -0.5)
    halfs = jnp.stack([jnp.concatenate([cos64, ident], axis=1), jnp.concatenate([hi64, zero64], axis=1),
                       jnp.concatenate([lo64, zero64], axis=1), jnp.concatenate([cos64, wscale], axis=1)])
    return full, halfs


def _dsa_call(x, bsz, seq, g_mix, w_in, w_out):
    t, dm = x.shape
    dq = ATTN_HEADS * ATTN_HEAD_DIM
    dqi = IDX_HEADS * IDX_DIM
    o_k, o_v, o_qi = dq, dq + ATTN_HEAD_DIM, dq + 2 * ATTN_HEAD_DIM
    o_ki = o_qi + dqi
    o_wi = o_ki + IDX_DIM
    wq = w_in[:, :o_k].astype(BF16)
    wqi = w_in[:, o_qi:o_ki].astype(BF16)
    wsm = jnp.concatenate([w_in[:, o_k:o_qi], w_in[:, o_ki:],
                           jnp.zeros((dm, LANES - IDX_DIM - IDX_HEADS), w_in.dtype)], axis=1).astype(BF16)
    full, halfs = _rope_tables(seq)
    tm = 256
    nt = seq // tm
    row = lambda i: (i, 0)
    tab = lambda i: (0, i % nt, 0)
    q, qi, k, v, ki, wi = pl.pallas_call(
        _dsa_proj_kernel,
        grid=(t // tm,),
        in_specs=[
            pl.BlockSpec((tm, dm), row), _resident((1, dm)), _resident(wq.shape), _resident(wqi.shape),
            _resident(wsm.shape),
            pl.BlockSpec((3, tm, LANES), tab), pl.BlockSpec((4, tm, LANES), tab),
        ],
        out_specs=[
            pl.BlockSpec((tm, dq), row), pl.BlockSpec((tm, dqi), row),
            pl.BlockSpec((tm, ATTN_HEAD_DIM), row), pl.BlockSpec((tm, ATTN_HEAD_DIM), row),
            pl.BlockSpec((tm, IDX_DIM), row), pl.BlockSpec((tm, LANES), row),
        ],
        out_shape=[
            jax.ShapeDtypeStruct((t, dq), BF16), jax.ShapeDtypeStruct((t, dqi), BF16),
            jax.ShapeDtypeStruct((t, ATTN_HEAD_DIM), BF16), jax.ShapeDtypeStruct((t, ATTN_HEAD_DIM), BF16),
            jax.ShapeDtypeStruct((t, IDX_DIM), BF16), jax.ShapeDtypeStruct((t, LANES), F32),
        ],
        compiler_params=_params("parallel"),
        name="dsa_proj",
    )(x, g_mix.reshape(1, dm), wq, wqi, wsm, full, halfs)

    qb = QUERY_BLOCK
    nb = seq // qb
    top_k = min(TOPK_MAX, seq // 4)
    blk = lambda b, n: (b * nb + n, 0)
    per_b = lambda b, n: (b, 0, 0)
    o = pl.pallas_call(
        functools.partial(_dsa_attn_kernel, top_k=top_k, n_span=math.gcd(nb, 4)),
        grid=(bsz, nb),
        in_specs=[
            pl.BlockSpec((qb, dq), blk), pl.BlockSpec((qb, dqi), blk), pl.BlockSpec((qb, LANES), blk),
            pl.BlockSpec((None, seq, ATTN_HEAD_DIM), per_b), pl.BlockSpec((None, seq, ATTN_HEAD_DIM), per_b),
            pl.BlockSpec((None, seq, IDX_DIM), per_b),
        ],
        out_specs=pl.BlockSpec((qb, dq), blk),
        out_shape=jax.ShapeDtypeStruct((t, dq), BF16),
        scratch_shapes=[pltpu.VMEM((qb, seq), jnp.int32), pltpu.VMEM((qb, seq), F32)],
        compiler_params=_params("parallel", "arbitrary"),
        name="dsa_attn",
    )(q, qi, wi, k.reshape(bsz, seq, -1), v.reshape(bsz, seq, -1), ki.reshape(bsz, seq, -1))

    tmo = 512
    return pl.pallas_call(
        _proj_res_kernel,
        grid=(t // tmo,),
        in_specs=[pl.BlockSpec((tmo, dq), row), _resident((dq, dm)), pl.BlockSpec((tmo, dm), row)],
        out_specs=pl.BlockSpec((tmo, dm), row),
        out_shape=jax.ShapeDtypeStruct((t, dm), F32),
        compiler_params=_params("parallel"),
        name="dsa_out",
    )(o, w_out.astype(BF16), x)


def kernel(x, ffn1_norm, ffn1_w_in, ffn1_w_out, mix_norm, ffn2_norm, ffn2_w_in, ffn2_w_out, ssd_in_proj, ssd_conv_w, ssd_conv_b, ssd_dt_bias, ssd_a_log, ssd_d, ssd_gate_norm, ssd_out_proj, s5_b_re, s5_b_im, s5_c_re, s5_c_im, s5_lam_re, s5_lam_im, s5_log_step, s5_d, s5_glu_w, s5_glu_b, dsa_in_proj, dsa_out_proj, final_norm):
    bsz, seq, dm = x.shape
    depth = ffn1_norm.shape[0]
    xt = x.reshape(bsz * seq, dm)
    for i in range(depth):
        j, kind = divmod(i, N_MIXERS)
        xt = _ffn_call(xt, ffn1_norm[i], ffn1_w_in[i].astype(BF16), ffn1_w_out[i].astype(BF16))
        if kind == 0:
            xt = _ssd_call(xt, bsz, seq, mix_norm[i], ssd_in_proj[j], ssd_conv_w[j], ssd_conv_b[j],
                           ssd_dt_bias[j], ssd_a_log[j], ssd_d[j], ssd_gate_norm[j], ssd_out_proj[j])
        elif kind == 1:
            xt = _s5_call(xt, bsz, seq, mix_norm[i], s5_b_re[j], s5_b_im[j], s5_c_re[j], s5_c_im[j],
                          s5_lam_re[j], s5_lam_im[j], s5_log_step[j], s5_d[j], s5_glu_w[j], s5_glu_b[j])
        else:
            xt = _dsa_call(xt, bsz, seq, mix_norm[i], dsa_in_proj[j], dsa_out_proj[j])
        post = final_norm if i == depth - 1 else None
        xt = _ffn_call(xt, ffn2_norm[i], ffn2_w_in[i].astype(BF16), ffn2_w_out[i].astype(BF16), post)
    return xt.reshape(bsz, seq, dm)
```

```python
import functools
import math

import jax
import jax.numpy as jnp
from jax import lax
from jax.experimental import pallas as pl
from jax.experimental.pallas import tpu as pltpu

F32 = jnp.float32
BF16 = jnp.bfloat16

NORM_EPS = 1e-6
N_MIXERS = 3

SSD_HEAD_DIM = 64
SSD_GROUPS = 8
SSD_HEADS_PER_GROUP = 4
SSD_HEADS = SSD_GROUPS * SSD_HEADS_PER_GROUP
SSD_STATE = 128
SSD_INNER = SSD_HEADS * SSD_HEAD_DIM
SSD_CONV = 4
SSD_CHUNK_MAX = 256
SSD_GROUP_WIDTH = SSD_HEADS_PER_GROUP * SSD_HEAD_DIM

S5_GROUP_WIDTH = 16
S5_STATE = 64
S5_SUB = 16

ATTN_HEADS = 16
ATTN_HEAD_DIM = 64
IDX_HEADS = 8
IDX_DIM = 64
TOPK_MAX = 256
QUERY_BLOCK = 128
ROPE_THETA = 10000.0
DSA_HEAD_GROUP = 4

LANES = 128
SUBLANES = 8
VMEM_LIMIT_BYTES = 56 * 1024 * 1024

_NT = (((1,), (1,)), ((), ()))


def _params(*semantics):
    return pltpu.CompilerParams(dimension_semantics=semantics, vmem_limit_bytes=VMEM_LIMIT_BYTES)


def _rms(x, g):
    ms = jnp.mean(x * x, axis=-1, keepdims=True)
    return x * lax.rsqrt(ms + NORM_EPS) * g


def _silu(x):
    return x * jax.nn.sigmoid(x)


def _softplus(x):
    return jnp.maximum(x, 0.0) + jnp.log1p(jnp.exp(-jnp.abs(x)))


def _resident(shape):
    zeros = (0,) * len(shape)
    return pl.BlockSpec(shape, lambda *_: zeros, pipeline_mode=pl.Buffered(1))


FFN_CHUNK = 256


def _ffn_kernel(x_ref, g_ref, win_ref, wout_ref, pg_ref, o_ref, *, post_norm):
    x = x_ref[...]
    h = _rms(x, g_ref[...]).astype(BF16)
    f = wout_ref.shape[0]
    acc = jnp.zeros(x.shape, F32)
    for j in range(f // FFN_CHUNK):
        lo = j * FFN_CHUNK
        gate = jnp.dot(h, win_ref[:, lo:lo + FFN_CHUNK], preferred_element_type=F32)
        up = jnp.dot(h, win_ref[:, f + lo:f + lo + FFN_CHUNK], preferred_element_type=F32)
        act = (_silu(gate) * up).astype(BF16)
        acc = acc + jnp.dot(act, wout_ref[lo:lo + FFN_CHUNK, :], preferred_element_type=F32)
    y = x + 0.5 * acc
    if post_norm:
        y = _rms(y, pg_ref[...])
    o_ref[...] = y


def _ffn_call(x, g, w_in, w_out, post_g=None):
    t, d = x.shape
    f = w_out.shape[0]
    tm = 1024
    assert t % tm == 0 and f % FFN_CHUNK == 0
    post_norm = post_g is not None
    pg = post_g if post_norm else g
    return pl.pallas_call(
        functools.partial(_ffn_kernel, post_norm=post_norm),
        grid=(t // tm,),
        in_specs=[
            pl.BlockSpec((tm, d), lambda i: (i, 0)),
            _resident((1, d)), _resident(w_in.shape), _resident(w_out.shape), _resident((1, d)),
        ],
        out_specs=pl.BlockSpec((tm, d), lambda i: (i, 0)),
        out_shape=jax.ShapeDtypeStruct((t, d), F32),
        compiler_params=_params("parallel"),
        name="ffn",
    )(x, g.reshape(1, d), w_in, w_out, pg.reshape(1, d))


def _split3(a):
    hi = a.astype(BF16)
    r1 = a - hi.astype(F32)
    mid = r1.astype(BF16)
    lo = (r1 - mid.astype(F32)).astype(BF16)
    return hi, mid, lo


def _ssd_kernel(x_ref, gmix_ref, wz_ref, wxbc_ref, wdt_ref, wdtt_ref, convw_ref, convb_ref,
                dtbc_ref, dtbr_ref, ac_ref, ar_ref, dvec_ref, gn_ref, wout_ref, selg_ref, selh_ref,
                o_ref,
                raw_scr, carry_scr, xs_scr, bm_scr, cm_scr, z_scr, yn_scr, dtc_scr, dtr_scr, state_scr,
                *, q, n_ch):
    pad = SUBLANES
    rows = n_ch * q
    gw = SSD_GROUP_WIDTH
    hd_dim = SSD_HEAD_DIM
    n_state = SSD_STATE
    hpg = SSD_HEADS_PER_GROUP

    @pl.when(pl.program_id(1) == 0)
    def _():
        state_scr[...] = jnp.zeros_like(state_scr)
        carry_scr[...] = jnp.zeros_like(carry_scr)

    x = x_ref[...]
    h = _rms(x, gmix_ref[...]).astype(BF16)
    z_scr[...] = jnp.dot(h, wz_ref[...], preferred_element_type=F32)

    cw = raw_scr.shape[1]
    n_x = SSD_INNER // cw
    n_b = SSD_GROUPS * n_state // cw
    for j in range(wxbc_ref.shape[1] // cw):
        sl = slice(j * cw, (j + 1) * cw)
        raw_scr[0:pad, :] = carry_scr[:, sl]
        raw_scr[pad:pad + rows, :] = jnp.dot(h, wxbc_ref[:, sl], preferred_element_type=F32)
        carry_scr[:, sl] = raw_scr[rows:rows + pad, :]
        w = convw_ref[:, sl]
        acc = raw_scr[pad - 3:pad - 3 + rows, :] * w[0:1]
        acc = acc + raw_scr[pad - 2:pad - 2 + rows, :] * w[1:2]
        acc = acc + raw_scr[pad - 1:pad - 1 + rows, :] * w[2:3]
        acc = acc + raw_scr[pad:pad + rows, :] * w[3:4]
        act = _silu(acc + convb_ref[:, sl])
        if j < n_x:
            xs_scr[:, sl] = act
        elif j < n_x + n_b:
            bm_scr[:, (j - n_x) * cw:(j - n_x + 1) * cw] = act
        else:
            cm_scr[:, (j - n_x - n_b) * cw:(j - n_x - n_b + 1) * cw] = act

    dtc_scr[...] = _softplus(jnp.dot(h, wdt_ref[...], preferred_element_type=F32) + dtbc_ref[...])
    dt_rows = _softplus(lax.dot_general(wdtt_ref[...], h, _NT, preferred_element_type=F32) + dtbr_ref[...])
    for ci in range(n_ch):
        dtr_scr[ci] = dt_rows[:, ci * q:(ci + 1) * q]
    row = lax.broadcasted_iota(jnp.int32, (q, q), 0)
    col = lax.broadcasted_iota(jnp.int32, (q, q), 1)
    causal = col <= row
    tri_low = jnp.where(causal, 1.0, 0.0).astype(BF16)
    tri_up = jnp.where(row <= col, 1.0, 0.0).astype(BF16)

    def expand(a, sel):
        return sum(jnp.dot(p, sel, preferred_element_type=F32) for p in _split3(a))

    def chunk(ci, carry):
        r = pl.ds(pl.multiple_of(ci * q, q), q)
        dt_r = dtr_scr[ci]
        a_c = dtc_scr[r, :] * ac_ref[...]
        a_r = dt_r * ar_ref[...]
        cs_c = sum(jnp.dot(tri_low, p, preferred_element_type=F32) for p in _split3(a_c))
        cs_r = sum(jnp.dot(p, tri_up, preferred_element_type=F32) for p in _split3(a_r))
        e_c = jnp.exp(cs_c)
        cs_end = cs_r[:, q - 1:q]
        f_r = dt_r * jnp.exp(cs_end - cs_r)
        e_end = jnp.exp(cs_end)
        for g in range(SSD_GROUPS):
            bg = bm_scr[r, g * n_state:(g + 1) * n_state].astype(BF16)
            cg = cm_scr[r, g * n_state:(g + 1) * n_state].astype(BF16)
            cb = lax.dot_general(cg, bg, _NT, preferred_element_type=F32)
            xs_g = xs_scr[r, g * gw:(g + 1) * gw]
            xs_gb = xs_g.astype(BF16)
            xs_gt = xs_g.T
            st_g = state_scr[g]
            y_off = lax.dot_general(cg, st_g.astype(BF16), _NT, preferred_element_type=F32)
            cs_cols = expand(cs_c, selh_ref[g])
            ys, ws, decs = [], [], []
            for k in range(hpg):
                hd = g * hpg + k
                cs_col = cs_cols[:, k * LANES:(k + 1) * LANES]
                seg = jnp.concatenate([cs_col] * (q // LANES), axis=1) - cs_r[hd:hd + 1, :]
                decay = jnp.exp(jnp.where(causal, seg, -jnp.inf))
                m = (cb * decay * dt_r[hd:hd + 1, :]).astype(BF16)
                ys.append(jnp.dot(m, xs_gb[:, k * hd_dim:(k + 1) * hd_dim], preferred_element_type=F32))
                ws.append(xs_gt[k * hd_dim:(k + 1) * hd_dim, :] * f_r[hd:hd + 1, :])
                decs.append(jnp.broadcast_to(e_end[hd:hd + 1, :], (hd_dim, n_state)))
            wt = jnp.concatenate(ws, axis=0)
            upd = jnp.dot(wt.astype(BF16), bg, preferred_element_type=F32)
            state_scr[g] = st_g * jnp.concatenate(decs, axis=0) + upd
            y_g = jnp.concatenate(ys, axis=1) + y_off * expand(e_c, selg_ref[g])
            y_g = y_g + dvec_ref[:, g * gw:(g + 1) * gw] * xs_g
            yg = y_g * _silu(z_scr[r, g * gw:(g + 1) * gw])
            yg = yg * lax.rsqrt(jnp.mean(yg * yg, axis=-1, keepdims=True) + NORM_EPS)
            yn_scr[r, g * gw:(g + 1) * gw] = (yg * gn_ref[:, g * gw:(g + 1) * gw]).astype(BF16)
        return carry

    lax.fori_loop(0, n_ch, chunk, 0)
    o_ref[...] = x + jnp.dot(yn_scr[...], wout_ref[...], preferred_element_type=F32)


def _ssd_call(x, bsz, seq, g_mix, in_proj, conv_w, conv_b, dt_bias, a_log, d, gate_norm, out_proj):
    t, dm = x.shape
    q = math.gcd(seq, SSD_CHUNK_MAX)
    nc = seq // q
    conv_dim = SSD_INNER + 2 * SSD_GROUPS * SSD_STATE
    wz = in_proj[:, :SSD_INNER].astype(BF16)
    wxbc = in_proj[:, SSD_INNER:SSD_INNER + conv_dim].astype(BF16)
    wdt = in_proj[:, SSD_INNER + conv_dim:]
    wdt_c = jnp.pad(wdt, ((0, 0), (0, LANES - SSD_HEADS))).astype(BF16)
    wdt_r = wdt.T.astype(BF16)
    a = -jnp.exp(a_log.astype(F32))
    lane_pad = (0, LANES - SSD_HEADS)
    head_of = jnp.arange(LANES)
    grp_heads = jnp.arange(SSD_GROUPS) * SSD_HEADS_PER_GROUP
    args = (
        x, g_mix.reshape(1, dm), wz, wxbc, wdt_c, wdt_r, conv_w, conv_b.reshape(1, conv_dim),
        jnp.pad(dt_bias, lane_pad).reshape(1, LANES), dt_bias.reshape(SSD_HEADS, 1),
        jnp.pad(a, lane_pad).reshape(1, LANES), a.reshape(SSD_HEADS, 1),
        jnp.repeat(d, SSD_HEAD_DIM).reshape(1, SSD_INNER), gate_norm.reshape(1, SSD_INNER),
        out_proj.astype(BF16),
        (head_of[:, None] == (grp_heads[:, None, None] + jnp.arange(SSD_GROUP_WIDTH) // SSD_HEAD_DIM)).astype(BF16),
        (head_of[:, None] == (grp_heads[:, None, None] + jnp.arange(SSD_HEADS_PER_GROUP * LANES) // LANES)).astype(BF16),
    )
    n_ch = math.gcd(nc, 2)
    steps = nc // n_ch
    rows = n_ch * q
    x_spec = pl.BlockSpec((rows, dm), lambda b, c: (b * steps + c, 0))
    in_specs = [x_spec] + [_resident(a_.shape) for a_ in args[1:]]
    return pl.pallas_call(
        functools.partial(_ssd_kernel, q=q, n_ch=n_ch),
        grid=(bsz, steps),
        in_specs=in_specs,
        out_specs=x_spec,
        out_shape=jax.ShapeDtypeStruct((t, dm), F32),
        scratch_shapes=[
            pltpu.VMEM((rows + 2 * SUBLANES, 4 * LANES), F32),
            pltpu.VMEM((SUBLANES, conv_dim), F32),
            pltpu.VMEM((rows, SSD_INNER), F32),
            pltpu.VMEM((rows, SSD_GROUPS * SSD_STATE), F32),
            pltpu.VMEM((rows, SSD_GROUPS * SSD_STATE), F32),
            pltpu.VMEM((rows, SSD_INNER), F32),
            pltpu.VMEM((rows, SSD_INNER), BF16),
            pltpu.VMEM((rows, LANES), F32),
            pltpu.VMEM((n_ch, SSD_HEADS, q), F32),
            pltpu.VMEM((SSD_GROUPS, SSD_GROUP_WIDTH, SSD_STATE), F32),
        ],
        compiler_params=_params("arbitrary", "arbitrary"),
        name="ssd_mixer",
    )(*args)


S5_GB = LANES // S5_GROUP_WIDTH


def _s5_pack_kernel(x_ref, g_ref, o_ref, h_scr):
    h = _rms(x_ref[...], g_ref[...])
    rows = o_ref.shape[1]
    for gb in range(o_ref.shape[0]):
        h_scr[gb] = h[:, gb * LANES:(gb + 1) * LANES]
    for s in range(S5_SUB):
        for gb in range(o_ref.shape[0]):
            piece = h_scr[gb, pl.ds(s, rows, stride=S5_SUB), :]
            o_ref[gb, :, s * LANES:(s + 1) * LANES] = piece.astype(o_ref.dtype)


def _s5_scan_kernel(u_ref, mop_ref, gop_ref, cop_ref, lam_ref, y_ref, g_scr, hp_scr, *, n_bt, n_sub):
    n_half = g_scr.shape[0] // 2
    tile = 2 * LANES
    u = u_ref[...]
    g_in = jnp.dot(u, gop_ref[...], preferred_element_type=F32)
    for c in range(2 * n_half):
        g_scr[c] = g_in[:, c * LANES:(c + 1) * LANES]
    lam = lam_ref[...]
    lr = [jnp.broadcast_to(lam[0:1, c * LANES:(c + 1) * LANES], (n_bt, LANES)) for c in range(n_half)]
    li = [jnp.broadcast_to(lam[1:2, c * LANES:(c + 1) * LANES], (n_bt, LANES)) for c in range(n_half)]
    hr = [jnp.zeros((n_bt, LANES), F32)] * n_half
    hi = [jnp.zeros((n_bt, LANES), F32)] * n_half
    for j in range(n_sub):
        rows = pl.ds(j, n_bt, stride=n_sub)
        for c in range(n_half):
            hp_scr[c, rows, :] = hr[c]
            hp_scr[n_half + c, rows, :] = hi[c]
            gr, gi = g_scr[c, rows, :], g_scr[n_half + c, rows, :]
            hr[c], hi[c] = lr[c] * hr[c] - li[c] * hi[c] + gr, lr[c] * hi[c] + li[c] * hr[c] + gi
    h_in = jnp.concatenate([hp_scr[c] for c in range(2 * n_half)], axis=1).astype(BF16)
    y_inter = jnp.dot(h_in, cop_ref[...], preferred_element_type=F32)
    n_t = u.shape[1] // tile
    for tp in range(n_t):
        k_hi = (tp + 1) * tile
        y_intra = jnp.dot(u[:, :k_hi], mop_ref[(n_t - 1 - tp) * tile:, :], preferred_element_type=F32)
        y_ref[:, tp * tile:(tp + 1) * tile] = y_inter[:, tp * tile:(tp + 1) * tile] + y_intra


def _s5_operators(b_re, b_im, c_re, c_im, lam_re, lam_im, log_step):
    hp = lax.Precision.HIGHEST
    s, gb, w = S5_SUB, S5_GB, S5_GROUP_WIDTH
    n_g, n_p = lam_re.shape
    n_blk = n_g // gb
    lam = lax.complex(lam_re.astype(F32), lam_im.astype(F32))
    step = jnp.exp(log_step.astype(F32))[:, None]
    tau = jnp.arange(s + 1, dtype=F32)[:, None, None]
    lam_pow = jnp.exp((lam * step)[None] * tau)
    b_bar = ((lam_pow[1] - 1.0) / lam)[..., None] * lax.complex(b_re.astype(F32), b_im.astype(F32))
    cc = lax.complex(c_re.astype(F32), c_im.astype(F32))
    eye = jnp.eye(gb, dtype=F32)
    kern = jnp.einsum('gcp,tgp,gpd->tgcd', cc, lam_pow[:s], b_bar, precision=hp).real
    bd = jnp.einsum('tagcd,gh->atgdhc', kern.reshape(s, n_blk, gb, w, w), eye).reshape(n_blk, s, gb * w, gb * w)
    bd = jnp.concatenate([bd, jnp.zeros_like(bd[:, :1])], axis=1)
    two = jnp.arange(2)
    tile_lag = jnp.arange(s // 2 - 1, -1, -1)
    lag = 2 * tile_lag[:, None, None] + two[None, None, :] - two[None, :, None]
    mop = bd[:, jnp.where(lag < 0, s, lag)]
    mop = mop.transpose(0, 1, 2, 4, 3, 5).reshape(n_blk, s * gb * w, 2 * gb * w)
    ti = jnp.arange(s)
    gm = (lam_pow[s - 1 - ti][:, :, :, None] * b_bar[None]).reshape(s, n_blk, gb, n_p, w)
    gop = jnp.stack([jnp.einsum('sagpd,gh->asgdhp', part, eye) for part in (gm.real, gm.imag)], axis=4)
    gop = gop.reshape(n_blk, s * gb * w, 2 * gb * n_p)
    em = (cc[None] * lam_pow[1:s + 1][:, :, None, :]).reshape(s, n_blk, gb, w, n_p)
    cop = jnp.stack([jnp.einsum('tagcp,gh->agpthc', part, eye) for part in (em.real, -em.imag)], axis=1)
    cop = cop.reshape(n_blk, 2 * gb * n_p, s * gb * w)
    lam_s = lam_pow[s].reshape(n_blk, gb * n_p)
    lam_blk = jnp.stack([lam_s.real, lam_s.imag], axis=1)
    return mop.astype(BF16), gop.astype(BF16), cop.astype(BF16), lam_blk


def _glu_kernel(x_ref, y_ref, g_ref, d_ref, w_ref, b_ref, o_ref, y_scr):
    x = x_ref[...]
    d_model = x.shape[1]
    rows = y_ref.shape[1]
    for s in range(S5_SUB):
        for gb in range(y_ref.shape[0]):
            y_scr[gb, pl.ds(s, rows, stride=S5_SUB), :] = y_ref[gb, :, s * LANES:(s + 1) * LANES]
    y = jnp.concatenate([y_scr[gb] for gb in range(y_ref.shape[0])], axis=1)
    h = _rms(x, g_ref[...])
    act = jax.nn.gelu(y + d_ref[...] * h).astype(BF16)
    vg = jnp.dot(act, w_ref[...], preferred_element_type=F32) + b_ref[...]
    o_ref[...] = x + vg[:, :d_model] * jax.nn.sigmoid(vg[:, d_model:])


def _s5_call(x, bsz, seq, g_mix, b_re, b_im, c_re, c_im, lam_re, lam_im, log_step, d, glu_w, glu_b):
    t, dm = x.shape
    s = S5_SUB
    n_blk = dm // LANES
    n_sub = seq // s
    rows = t // s
    width = s * LANES
    state_w = 2 * S5_GB * S5_STATE
    n_bt = 4
    tm = 512
    assert seq % s == 0 and bsz % n_bt == 0 and t % tm == 0
    mop, gop, cop, lam_blk = _s5_operators(b_re, b_im, c_re, c_im, lam_re, lam_im, log_step)
    tok = lambda i: (i, 0)
    packed = lambda i: (0, i, 0)
    u = pl.pallas_call(
        _s5_pack_kernel,
        grid=(t // tm,),
        in_specs=[pl.BlockSpec((tm, dm), tok), _resident((1, dm))],
        out_specs=pl.BlockSpec((n_blk, tm // s, width), packed),
        out_shape=jax.ShapeDtypeStruct((n_blk, rows, width), BF16),
        scratch_shapes=[pltpu.VMEM((n_blk, tm, LANES), F32)],
        compiler_params=_params("parallel"),
        name="s5_pack",
    )(x, g_mix.reshape(1, dm))
    rt = n_bt * n_sub
    per_blk = lambda a, r: (a, 0, 0)
    once = pl.Buffered(1)
    y = pl.pallas_call(
        functools.partial(_s5_scan_kernel, n_bt=n_bt, n_sub=n_sub),
        grid=(n_blk, rows // rt),
        in_specs=[
            pl.BlockSpec((None, rt, width), lambda a, r: (a, r, 0)),
            pl.BlockSpec((None, width, 2 * LANES), per_blk, pipeline_mode=once),
            pl.BlockSpec((None, width, state_w), per_blk, pipeline_mode=once),
            pl.BlockSpec((None, state_w, width), per_blk, pipeline_mode=once),
            pl.BlockSpec((None, 2, state_w // 2), per_blk),
        ],
        out_specs=pl.BlockSpec((None, rt, width), lambda a, r: (a, r, 0)),
        out_shape=jax.ShapeDtypeStruct((n_blk, rows, width), F32),
        scratch_shapes=[pltpu.VMEM((state_w // LANES, rt, LANES), F32)] * 2,
        compiler_params=_params("arbitrary", "arbitrary"),
        name="s5_scan",
    )(u, mop, gop, cop, lam_blk)
    return pl.pallas_call(
        _glu_kernel,
        grid=(t // tm,),
        in_specs=[
            pl.BlockSpec((tm, dm), tok),
            pl.BlockSpec((n_blk, tm // s, width), packed),
            _resident((1, dm)), _resident((1, dm)), _resident((dm, 2 * dm)), _resident((1, 2 * dm)),
        ],
        out_specs=pl.BlockSpec((tm, dm), tok),
        out_shape=jax.ShapeDtypeStruct((t, dm), F32),
        scratch_shapes=[pltpu.VMEM((n_blk, tm, LANES), F32)],
        compiler_params=_params("parallel"),
        name="s5_glu",
    )(x, y, g_mix.reshape(1, dm), d.reshape(1, dm), glu_w.astype(BF16), glu_b.reshape(1, 2 * dm))


def _rope128(tile, cos, sin_hi, sin_lo):
    return tile * cos + pltpu.roll(tile, 32, 1) * sin_hi + pltpu.roll(tile, LANES - 32, 1) * sin_lo


def _dsa_proj_kernel(x_ref, g_ref, wq_ref, wqi_ref, wsm_ref, tab_ref, tabh_ref,
                     q_ref, qi_ref, k_ref, v_ref, ki_ref, wi_ref):
    h = _rms(x_ref[...], g_ref[...]).astype(BF16)
    cos, s_hi, s_lo = tab_ref[0], tab_ref[1], tab_ref[2]
    q = jnp.dot(h, wq_ref[...], preferred_element_type=F32)
    scale = ATTN_HEAD_DIM ** -0.5
    for j in range(q.shape[1] // LANES):
        sl = slice(j * LANES, (j + 1) * LANES)
        q_ref[:, sl] = (_rope128(q[:, sl], cos, s_hi, s_lo) * scale).astype(BF16)
    qi = jnp.dot(h, wqi_ref[...], preferred_element_type=F32)
    for j in range(qi.shape[1] // LANES):
        sl = slice(j * LANES, (j + 1) * LANES)
        qi_ref[:, sl] = _rope128(qi[:, sl], cos, s_hi, s_lo).astype(BF16)
    sm = jnp.dot(h, wsm_ref[...], preferred_element_type=F32)
    kv = _rope128(sm[:, :LANES], tabh_ref[0], tabh_ref[1], tabh_ref[2])
    kw = _rope128(sm[:, LANES:], tabh_ref[3], tabh_ref[1], tabh_ref[2])
    k_ref[...] = kv[:, :ATTN_HEAD_DIM].astype(BF16)
    v_ref[...] = kv[:, ATTN_HEAD_DIM:].astype(BF16)
    ki_ref[...] = kw[:, :IDX_DIM].astype(BF16)
    wi_ref[...] = kw


def _dsa_attn_kernel(q_ref, qi_ref, wi_ref, k_ref, v_ref, ki_ref, o_ref, key_scr, bias_scr, *, top_k, n_span):
    n = pl.program_id(1)
    per_span = pl.num_programs(1) // n_span
    span_keys = key_scr.shape[1] // n_span
    for i in range(n_span):
        @pl.when(n // per_span == i)
        def _(i=i):
            _dsa_attn_block(q_ref, qi_ref, wi_ref, k_ref, v_ref, ki_ref, o_ref, key_scr, bias_scr,
                            top_k=top_k, seq=(i + 1) * span_keys)


def _dsa_attn_block(q_ref, qi_ref, wi_ref, k_ref, v_ref, ki_ref, o_ref, key_full, bias_full, *, top_k, seq):
    n = pl.program_id(1)
    qb = key_full.shape[0]
    key_scr = key_full.at[:, 0:seq]
    bias_scr = bias_full.at[:, 0:seq]
    kib = ki_ref[0:seq, :]
    wi = wi_ref[...]
    qis = jnp.concatenate([qi_ref[:, hh * IDX_DIM:(hh + 1) * IDX_DIM] for hh in range(IDX_HEADS)], axis=0)
    sc = lax.dot_general(qis, kib, _NT, preferred_element_type=F32)
    idx = jnp.zeros((qb, seq), F32)
    for hh in range(IDX_HEADS):
        idx = idx + jnp.maximum(sc[hh * qb:(hh + 1) * qb], 0.0) * wi[:, IDX_DIM + hh:IDX_DIM + hh + 1]
    t_pos = n * qb + lax.broadcasted_iota(jnp.int32, (qb, 1), 0)
    s_pos = lax.broadcasted_iota(jnp.int32, (1, seq), 1)
    causal = s_pos <= t_pos
    idx = jnp.where(causal, idx, -jnp.inf)
    bits = lax.bitcast_convert_type(idx, jnp.int32)
    key_scr[...] = jnp.where(bits < 0, bits ^ jnp.int32(0x7FFFFFFF), bits)

    def count(pred):
        return jnp.sum(jnp.where(pred, 1.0, 0.0), axis=1, keepdims=True)

    int_min = jnp.int32(-2 ** 31)
    kf = float(top_k)
    thr0 = jnp.where(count(key_scr[...] >= 0) >= kf, jnp.int32(0), int_min)

    def thr_step(i, thr):
        cand = thr + lax.shift_left(jnp.int32(1), jnp.int32(30) - i)
        return jnp.where(count(key_scr[...] >= cand) >= kf, cand, thr)

    thr = lax.fori_loop(0, 31, thr_step, thr0)
    keys = key_scr[...]
    n_ge = count(keys >= thr)

    def tie_cut():
        need = kf - count(key_scr[...] > thr)
        n_bits = (seq - 1).bit_length()

        def cut_step(i, cut):
            cand = cut + lax.shift_left(jnp.int32(1), jnp.int32(n_bits - 1) - i)
            return jnp.where(count((key_scr[...] == thr) & (s_pos < cand)) < need, cand, cut)

        return lax.fori_loop(0, n_bits, cut_step, jnp.zeros((qb, 1), jnp.int32))

    cut = lax.cond(jnp.max(n_ge) > kf, tie_cut, lambda: jnp.full((qb, 1), seq, jnp.int32))
    sel = (keys > thr) | ((keys == thr) & (s_pos <= cut))
    bias_scr[...] = jnp.where(sel & causal, 0.0, -jnp.inf)

    kb = k_ref[0:seq, :]
    vb = v_ref[0:seq, :]
    hg = DSA_HEAD_GROUP
    dh = ATTN_HEAD_DIM
    for g in range(ATTN_HEADS // hg):
        qs = jnp.concatenate([q_ref[:, (g * hg + j) * dh:(g * hg + j + 1) * dh] for j in range(hg)], axis=0)
        lg = lax.dot_general(qs, kb, _NT, preferred_element_type=F32)
        lg = (lg.reshape(hg, qb, seq) + bias_scr[...][None]).reshape(hg * qb, seq)
        p = jnp.exp(lg - jnp.max(lg, axis=1, keepdims=True))
        den = jnp.sum(p, axis=1, keepdims=True)
        og = jnp.dot(p.astype(BF16), vb, preferred_element_type=F32) / den
        for j in range(0, hg, 2):
            pair = jnp.concatenate([og[j * qb:(j + 1) * qb], og[(j + 1) * qb:(j + 2) * qb]], axis=1)
            lo = (g * hg + j) * dh
            o_ref[:, lo:lo + 2 * dh] = pair.astype(BF16)


def _proj_res_kernel(a_ref, w_ref, x_ref, o_ref):
    o_ref[...] = x_ref[...] + jnp.dot(a_ref[...], w_ref[...], preferred_element_type=F32)


def _rope_tables(seq):
    half = ATTN_HEAD_DIM // 2
    inv = ROPE_THETA ** (-jnp.arange(half, dtype=F32) / half)
    ang = jnp.arange(seq, dtype=F32)[:, None] * inv
    cos, sin = jnp.cos(ang), jnp.sin(ang)
    zero, one = jnp.zeros_like(cos), jnp.ones_like(cos)
    cos64 = jnp.concatenate([cos, cos], axis=1)
    hi64 = jnp.concatenate([zero, sin], axis=1)
    lo64 = jnp.concatenate([-sin, zero], axis=1)
    full = jnp.stack([jnp.tile(t, (1, 2)) for t in (cos64, hi64, lo64)])
    ident = jnp.concatenate([one, one], axis=1)
    zero64 = jnp.concatenate([zero, zero], axis=1)
    wscale = ident * (IDX_HEADS ** -0.5 * IDX_DIM ** -0.5)
    halfs = jnp.stack([jnp.concatenate([cos64, ident], axis=1), jnp.concatenate([hi64, zero64], axis=1),
                       jnp.concatenate([lo64, zero64], axis=1), jnp.concatenate([cos64, wscale], axis=1)])
    return full, halfs


def _dsa_call(x, bsz, seq, g_mix, w_in, w_out):
    t, dm = x.shape
    dq = ATTN_HEADS * ATTN_HEAD_DIM
    dqi = IDX_HEADS * IDX_DIM
    o_k, o_v, o_qi = dq, dq + ATTN_HEAD_DIM, dq + 2 * ATTN_HEAD_DIM
    o_ki = o_qi + dqi
    o_wi = o_ki + IDX_DIM
    wq = w_in[:, :o_k].astype(BF16)
    wqi = w_in[:, o_qi:o_ki].astype(BF16)
    wsm = jnp.concatenate([w_in[:, o_k:o_qi], w_in[:, o_ki:],
                           jnp.zeros((dm, LANES - IDX_DIM - IDX_HEADS), w_in.dtype)], axis=1).astype(BF16)
    full, halfs = _rope_tables(seq)
    tm = 256
    nt = seq // tm
    row = lambda i: (i, 0)
    tab = lambda i: (0, i % nt, 0)
    q, qi, k, v, ki, wi = pl.pallas_call(
        _dsa_proj_kernel,
        grid=(t // tm,),
        in_specs=[
            pl.BlockSpec((tm, dm), row), _resident((1, dm)), _resident(wq.shape), _resident(wqi.shape),
            _resident(wsm.shape),
            pl.BlockSpec((3, tm, LANES), tab), pl.BlockSpec((4, tm, LANES), tab),
        ],
        out_specs=[
            pl.BlockSpec((tm, dq), row), pl.BlockSpec((tm, dqi), row),
            pl.BlockSpec((tm, ATTN_HEAD_DIM), row), pl.BlockSpec((tm, ATTN_HEAD_DIM), row),
            pl.BlockSpec((tm, IDX_DIM), row), pl.BlockSpec((tm, LANES), row),
        ],
        out_shape=[
            jax.ShapeDtypeStruct((t, dq), BF16), jax.ShapeDtypeStruct((t, dqi), BF16),
            jax.ShapeDtypeStruct((t, ATTN_HEAD_DIM), BF16), jax.ShapeDtypeStruct((t, ATTN_HEAD_DIM), BF16),
            jax.ShapeDtypeStruct((t, IDX_DIM), BF16), jax.ShapeDtypeStruct((t, LANES), F32),
        ],
        compiler_params=_params("parallel"),
        name="dsa_proj",
    )(x, g_mix.reshape(1, dm), wq, wqi, wsm, full, halfs)

    qb = QUERY_BLOCK
    nb = seq // qb
    top_k = min(TOPK_MAX, seq // 4)
    blk = lambda b, n: (b * nb + n, 0)
    per_b = lambda b, n: (b, 0, 0)
    o = pl.pallas_call(
        functools.partial(_dsa_attn_kernel, top_k=top_k, n_span=math.gcd(nb, 4)),
        grid=(bsz, nb),
        in_specs=[
            pl.BlockSpec((qb, dq), blk), pl.BlockSpec((qb, dqi), blk), pl.BlockSpec((qb, LANES), blk),
            pl.BlockSpec((None, seq, ATTN_HEAD_DIM), per_b), pl.BlockSpec((None, seq, ATTN_HEAD_DIM), per_b),
            pl.BlockSpec((None, seq, IDX_DIM), per_b),
        ],
        out_specs=pl.BlockSpec((qb, dq), blk),
        out_shape=jax.ShapeDtypeStruct((t, dq), BF16),
        scratch_shapes=[pltpu.VMEM((qb, seq), jnp.int32), pltpu.VMEM((qb, seq), F32)],
        compiler_params=_params("parallel", "arbitrary"),
        name="dsa_attn",
    )(q, qi, wi, k.reshape(bsz, seq, -1), v.reshape(bsz, seq, -1), ki.reshape(bsz, seq, -1))

    tmo = 512
    return pl.pallas_call(
        _proj_res_kernel,
        grid=(t // tmo,),
        in_specs=[pl.BlockSpec((tmo, dq), row), _resident((dq, dm)), pl.BlockSpec((tmo, dm), row)],
        out_specs=pl.BlockSpec((tmo, dm), row),
        out_shape=jax.ShapeDtypeStruct((t, dm), F32),
        compiler_params=_params("parallel"),
        name="dsa_out",
    )(o, w_out.astype(BF16), x)


def kernel(x, ffn1_norm, ffn1_w_in, ffn1_w_out, mix_norm, ffn2_norm, ffn2_w_in, ffn2_w_out, ssd_in_proj, ssd_conv_w, ssd_conv_b, ssd_dt_bias, ssd_a_log, ssd_d, ssd_gate_norm, ssd_out_proj, s5_b_re, s5_b_im, s5_c_re, s5_c_im, s5_lam_re, s5_lam_im, s5_log_step, s5_d, s5_glu_w, s5_glu_b, dsa_in_proj, dsa_out_proj, final_norm):
    bsz, seq, dm = x.shape
    depth = ffn1_norm.shape[0]
    xt = x.reshape(bsz * seq, dm)
    for i in range(depth):
        j, kind = divmod(i, N_MIXERS)
        xt = _ffn_call(xt, ffn1_norm[i], ffn1_w_in[i].astype(BF16), ffn1_w_out[i].astype(BF16))
        if kind == 0:
            xt = _ssd_call(xt, bsz, seq, mix_norm[i], ssd_in_proj[j], ssd_conv_w[j], ssd_conv_b[j],
                           ssd_dt_bias[j], ssd_a_log[j], ssd_d[j], ssd_gate_norm[j], ssd_out_proj[j])
        elif kind == 1:
            xt = _s5_call(xt, bsz, seq, mix_norm[i], s5_b_re[j], s5_b_im[j], s5_c_re[j], s5_c_im[j],
                          s5_lam_re[j], s5_lam_im[j], s5_log_step[j], s5_d[j], s5_glu_w[j], s5_glu_b[j])
        else:
            xt = _dsa_call(xt, bsz, seq, mix_norm[i], dsa_in_proj[j], dsa_out_proj[j])
        post = final_norm if i == depth - 1 else None
        xt = _ffn_call(xt, ffn2_norm[i], ffn2_w_in[i].astype(BF16), ffn2_w_out[i].astype(BF16), post)
    return xt.reshape(bsz, seq, dm)
```

```python
import functools
import math

import jax
import jax.numpy as jnp
from jax import lax
from jax.experimental import pallas as pl
from jax.experimental.pallas import tpu as pltpu

F32 = jnp.float32
BF16 = jnp.bfloat16

NORM_EPS = 1e-6
N_MIXERS = 3

SSD_HEAD_DIM = 64
SSD_GROUPS = 8
SSD_HEADS_PER_GROUP = 4
SSD_HEADS = SSD_GROUPS * SSD_HEADS_PER_GROUP
SSD_STATE = 128
SSD_INNER = SSD_HEADS * SSD_HEAD_DIM
SSD_CONV = 4
SSD_CHUNK_MAX = 256
SSD_GROUP_WIDTH = SSD_HEADS_PER_GROUP * SSD_HEAD_DIM

S5_GROUP_WIDTH = 16
S5_STATE = 64
S5_SUB = 16

ATTN_HEADS = 16
ATTN_HEAD_DIM = 64
IDX_HEADS = 8
IDX_DIM = 64
TOPK_MAX = 256
QUERY_BLOCK = 128
ROPE_THETA = 10000.0
DSA_HEAD_GROUP = 4

LANES = 128
SUBLANES = 8
VMEM_LIMIT_BYTES = 56 * 1024 * 1024

_NT = (((1,), (1,)), ((), ()))


def _params(*semantics):
    return pltpu.CompilerParams(dimension_semantics=semantics, vmem_limit_bytes=VMEM_LIMIT_BYTES)


def _rms(x, g):
    ms = jnp.mean(x * x, axis=-1, keepdims=True)
    return x * lax.rsqrt(ms + NORM_EPS) * g


def _silu(x):
    return x * jax.nn.sigmoid(x)


def _softplus(x):
    return jnp.maximum(x, 0.0) + jnp.log1p(jnp.exp(-jnp.abs(x)))


def _resident(shape):
    zeros = (0,) * len(shape)
    return pl.BlockSpec(shape, lambda *_: zeros, pipeline_mode=pl.Buffered(1))


FFN_CHUNK = 256


def _ffn_kernel(x_ref, g_ref, win_ref, wout_ref, pg_ref, o_ref, *, post_norm):
    x = x_ref[...]
    h = _rms(x, g_ref[...]).astype(BF16)
    f = wout_ref.shape[0]
    acc = jnp.zeros(x.shape, F32)
    for j in range(f // FFN_CHUNK):
        lo = j * FFN_CHUNK
        gate = jnp.dot(h, win_ref[:, lo:lo + FFN_CHUNK], preferred_element_type=F32)
        up = jnp.dot(h, win_ref[:, f + lo:f + lo + FFN_CHUNK], preferred_element_type=F32)
        act = (_silu(gate) * up).astype(BF16)
        acc = acc + jnp.dot(act, wout_ref[lo:lo + FFN_CHUNK, :], preferred_element_type=F32)
    y = x + 0.5 * acc
    if post_norm:
        y = _rms(y, pg_ref[...])
    o_ref[...] = y


def _ffn_call(x, g, w_in, w_out, post_g=None):
    t, d = x.shape
    f = w_out.shape[0]
    tm = 1024
    assert t % tm == 0 and f % FFN_CHUNK == 0
    post_norm = post_g is not None
    pg = post_g if post_norm else g
    return pl.pallas_call(
        functools.partial(_ffn_kernel, post_norm=post_norm),
        grid=(t // tm,),
        in_specs=[
            pl.BlockSpec((tm, d), lambda i: (i, 0)),
            _resident((1, d)), _resident(w_in.shape), _resident(w_out.shape), _resident((1, d)),
        ],
        out_specs=pl.BlockSpec((tm, d), lambda i: (i, 0)),
        out_shape=jax.ShapeDtypeStruct((t, d), F32),
        compiler_params=_params("parallel"),
        name="ffn",
    )(x, g.reshape(1, d), w_in, w_out, pg.reshape(1, d))


def _split3(a):
    hi = a.astype(BF16)
    r1 = a - hi.astype(F32)
    mid = r1.astype(BF16)
    lo = (r1 - mid.astype(F32)).astype(BF16)
    return hi, mid, lo


def _ssd_kernel(x_ref, gmix_ref, wz_ref, wxbc_ref, wdt_ref, wdtt_ref, convw_ref, convb_ref,
                dtbc_ref, dtbr_ref, ac_ref, ar_ref, dvec_ref, gn_ref, wout_ref, selg_ref, selh_ref,
                o_ref,
                raw_scr, carry_scr, xs_scr, bm_scr, cm_scr, z_scr, yn_scr, dtc_scr, dtr_scr, state_scr,
                *, q, n_ch):
    pad = SUBLANES
    rows = n_ch * q
    gw = SSD_GROUP_WIDTH
    hd_dim = SSD_HEAD_DIM
    n_state = SSD_STATE
    hpg = SSD_HEADS_PER_GROUP

    @pl.when(pl.program_id(1) == 0)
    def _():
        state_scr[...] = jnp.zeros_like(state_scr)
        carry_scr[...] = jnp.zeros_like(carry_scr)

    x = x_ref[...]
    h = _rms(x, gmix_ref[...]).astype(BF16)
    z_scr[...] = jnp.dot(h, wz_ref[...], preferred_element_type=F32)

    cw = raw_scr.shape[1]
    n_x = SSD_INNER // cw
    n_b = SSD_GROUPS * n_state // cw
    for j in range(wxbc_ref.shape[1] // cw):
        sl = slice(j * cw, (j + 1) * cw)
        raw = raw_scr
        raw[0:pad, :] = carry_scr[:, sl]
        raw[pad:pad + rows, :] = jnp.dot(h, wxbc_ref[:, sl], preferred_element_type=F32)
        carry_scr[:, sl] = raw[rows:rows + pad, :]
        w = convw_ref[:, sl]
        acc = raw[pad - 3:pad - 3 + rows, :] * w[0:1]
        acc = acc + raw[pad - 2:pad - 2 + rows, :] * w[1:2]
        acc = acc + raw[pad - 1:pad - 1 + rows, :] * w[2:3]
        acc = acc + raw[pad:pad + rows, :] * w[3:4]
        act = _silu(acc + convb_ref[:, sl])
        if j < n_x:
            xs_scr[:, sl] = act
        elif j < n_x + n_b:
            bm_scr[:, (j - n_x) * cw:(j - n_x + 1) * cw] = act
        else:
            cm_scr[:, (j - n_x - n_b) * cw:(j - n_x - n_b + 1) * cw] = act

    dtc_scr[...] = _softplus(jnp.dot(h, wdt_ref[...], preferred_element_type=F32) + dtbc_ref[...])
    dt_rows = _softplus(lax.dot_general(wdtt_ref[...], h, _NT, preferred_element_type=F32) + dtbr_ref[...])
    for ci in range(n_ch):
        dtr_scr[ci] = dt_rows[:, ci * q:(ci + 1) * q]
    row = lax.broadcasted_iota(jnp.int32, (q, q), 0)
    col = lax.broadcasted_iota(jnp.int32, (q, q), 1)
    causal = col <= row
    tri_low = jnp.where(causal, 1.0, 0.0).astype(BF16)
    tri_up = jnp.where(row <= col, 1.0, 0.0).astype(BF16)

    def expand(a, sel):
        return sum(jnp.dot(p, sel, preferred_element_type=F32) for p in _split3(a))

    def chunk(ci, carry):
        r = pl.ds(pl.multiple_of(ci * q, q), q)
        dt_r = dtr_scr[ci]
        a_c = dtc_scr[r, :] * ac_ref[...]
        a_r = dt_r * ar_ref[...]
        cs_c = sum(jnp.dot(tri_low, p, preferred_element_type=F32) for p in _split3(a_c))
        cs_r = sum(jnp.dot(p, tri_up, preferred_element_type=F32) for p in _split3(a_r))
        e_c = jnp.exp(cs_c)
        cs_end = cs_r[:, q - 1:q]
        f_r = dt_r * jnp.exp(cs_end - cs_r)
        e_end = jnp.exp(cs_end)
        for g in range(SSD_GROUPS):
            bg = bm_scr[r, g * n_state:(g + 1) * n_state].astype(BF16)
            cg = cm_scr[r, g * n_state:(g + 1) * n_state].astype(BF16)
            cb = lax.dot_general(cg, bg, _NT, preferred_element_type=F32)
            xs_g = xs_scr[r, g * gw:(g + 1) * gw]
            xs_gb = xs_g.astype(BF16)
            xs_gt = xs_g.T
            st_g = state_scr[g]
            y_off = lax.dot_general(cg, st_g.astype(BF16), _NT, preferred_element_type=F32)
            cs_cols = expand(cs_c, selh_ref[g])
            ys, ws, decs = [], [], []
            for k in range(hpg):
                hd = g * hpg + k
                cs_col = cs_cols[:, k * LANES:(k + 1) * LANES]
                seg = jnp.concatenate([cs_col] * (q // LANES), axis=1) - cs_r[hd:hd + 1, :]
                decay = jnp.exp(jnp.where(causal, seg, -jnp.inf))
                m = (cb * decay * dt_r[hd:hd + 1, :]).astype(BF16)
                ys.append(jnp.dot(m, xs_gb[:, k * hd_dim:(k + 1) * hd_dim], preferred_element_type=F32))
                ws.append(xs_gt[k * hd_dim:(k + 1) * hd_dim, :] * f_r[hd:hd + 1, :])
                decs.append(jnp.broadcast_to(e_end[hd:hd + 1, :], (hd_dim, n_state)))
            wt = jnp.concatenate(ws, axis=0)
            upd = jnp.dot(wt.astype(BF16), bg, preferred_element_type=F32)
            state_scr[g] = st_g * jnp.concatenate(decs, axis=0) + upd
            y_g = jnp.concatenate(ys, axis=1) + y_off * expand(e_c, selg_ref[g])
            y_g = y_g + dvec_ref[:, g * gw:(g + 1) * gw] * xs_g
            yg = y_g * _silu(z_scr[r, g * gw:(g + 1) * gw])
            yg = yg * lax.rsqrt(jnp.mean(yg * yg, axis=-1, keepdims=True) + NORM_EPS)
            yn_scr[r, g * gw:(g + 1) * gw] = (yg * gn_ref[:, g * gw:(g + 1) * gw]).astype(BF16)
        return carry

    lax.fori_loop(0, n_ch, chunk, 0)
    o_ref[...] = x + jnp.dot(yn_scr[...], wout_ref[...], preferred_element_type=F32)


def _ssd_call(x, bsz, seq, g_mix, in_proj, conv_w, conv_b, dt_bias, a_log, d, gate_norm, out_proj):
    t, dm = x.shape
    q = math.gcd(seq, SSD_CHUNK_MAX)
    nc = seq // q
    conv_dim = SSD_INNER + 2 * SSD_GROUPS * SSD_STATE
    wz = in_proj[:, :SSD_INNER].astype(BF16)
    wxbc = in_proj[:, SSD_INNER:SSD_INNER + conv_dim].astype(BF16)
    wdt = in_proj[:, SSD_INNER + conv_dim:]
    wdt_c = jnp.pad(wdt, ((0, 0), (0, LANES - SSD_HEADS))).astype(BF16)
    wdt_r = wdt.T.astype(BF16)
    a = -jnp.exp(a_log.astype(F32))
    lane_pad = (0, LANES - SSD_HEADS)
    head_of = jnp.arange(LANES)
    grp_heads = jnp.arange(SSD_GROUPS) * SSD_HEADS_PER_GROUP
    args = (
        x, g_mix.reshape(1, dm), wz, wxbc, wdt_c, wdt_r, conv_w, conv_b.reshape(1, conv_dim),
        jnp.pad(dt_bias, lane_pad).reshape(1, LANES), dt_bias.reshape(SSD_HEADS, 1),
        jnp.pad(a, lane_pad).reshape(1, LANES), a.reshape(SSD_HEADS, 1),
        jnp.repeat(d, SSD_HEAD_DIM).reshape(1, SSD_INNER), gate_norm.reshape(1, SSD_INNER),
        out_proj.astype(BF16),
        (head_of[:, None] == (grp_heads[:, None, None] + jnp.arange(SSD_GROUP_WIDTH) // SSD_HEAD_DIM)).astype(BF16),
        (head_of[:, None] == (grp_heads[:, None, None] + jnp.arange(SSD_HEADS_PER_GROUP * LANES) // LANES)).astype(BF16),
    )
    n_ch = math.gcd(nc, 2)
    steps = nc // n_ch
    rows = n_ch * q
    x_spec = pl.BlockSpec((rows, dm), lambda b, c: (b * steps + c, 0))
    in_specs = [x_spec] + [_resident(a_.shape) for a_ in args[1:]]
    return pl.pallas_call(
        functools.partial(_ssd_kernel, q=q, n_ch=n_ch),
        grid=(bsz, steps),
        in_specs=in_specs,
        out_specs=x_spec,
        out_shape=jax.ShapeDtypeStruct((t, dm), F32),
        scratch_shapes=[
            pltpu.VMEM((rows + 2 * SUBLANES, 4 * LANES), F32),
            pltpu.VMEM((SUBLANES, conv_dim), F32),
            pltpu.VMEM((rows, SSD_INNER), F32),
            pltpu.VMEM((rows, SSD_GROUPS * SSD_STATE), F32),
            pltpu.VMEM((rows, SSD_GROUPS * SSD_STATE), F32),
            pltpu.VMEM((rows, SSD_INNER), F32),
            pltpu.VMEM((rows, SSD_INNER), BF16),
            pltpu.VMEM((rows, LANES), F32),
            pltpu.VMEM((n_ch, SSD_HEADS, q), F32),
            pltpu.VMEM((SSD_GROUPS, SSD_GROUP_WIDTH, SSD_STATE), F32),
        ],
        compiler_params=_params("arbitrary", "arbitrary"),
        name="ssd_mixer",
    )(*args)


S5_GB = LANES // S5_GROUP_WIDTH


def _s5_pack_kernel(x_ref, g_ref, o_ref, h_scr):
    h = _rms(x_ref[...], g_ref[...])
    rows = o_ref.shape[1]
    for gb in range(o_ref.shape[0]):
        h_scr[gb] = h[:, gb * LANES:(gb + 1) * LANES]
    for s in range(S5_SUB):
        for gb in range(o_ref.shape[0]):
            piece = h_scr[gb, pl.ds(s, rows, stride=S5_SUB), :]
            o_ref[gb, :, s * LANES:(s + 1) * LANES] = piece.astype(o_ref.dtype)


def _s5_scan_kernel(u_ref, mop_ref, gop_ref, cop_ref, lam_ref, y_ref, g_scr, hp_scr, *, n_bt, n_sub):
    n_half = g_scr.shape[0] // 2
    tile = 2 * LANES
    u = u_ref[...]
    g_in = jnp.dot(u, gop_ref[...], preferred_element_type=F32)
    for c in range(2 * n_half):
        g_scr[c] = g_in[:, c * LANES:(c + 1) * LANES]
    lam = lam_ref[...]
    lr = [jnp.broadcast_to(lam[0:1, c * LANES:(c + 1) * LANES], (n_bt, LANES)) for c in range(n_half)]
    li = [jnp.broadcast_to(lam[1:2, c * LANES:(c + 1) * LANES], (n_bt, LANES)) for c in range(n_half)]
    hr = [jnp.zeros((n_bt, LANES), F32)] * n_half
    hi = [jnp.zeros((n_bt, LANES), F32)] * n_half
    for j in range(n_sub):
        rows = pl.ds(j, n_bt, stride=n_sub)
        for c in range(n_half):
            hp_scr[c, rows, :] = hr[c]
            hp_scr[n_half + c, rows, :] = hi[c]
            gr, gi = g_scr[c, rows, :], g_scr[n_half + c, rows, :]
            hr[c], hi[c] = lr[c] * hr[c] - li[c] * hi[c] + gr, lr[c] * hi[c] + li[c] * hr[c] + gi
    h_in = jnp.concatenate([hp_scr[c] for c in range(2 * n_half)], axis=1).astype(BF16)
    y_inter = jnp.dot(h_in, cop_ref[...], preferred_element_type=F32)
    n_t = u.shape[1] // tile
    for tp in range(n_t):
        k_hi = (tp + 1) * tile
        y_intra = jnp.dot(u[:, :k_hi], mop_ref[(n_t - 1 - tp) * tile:, :], preferred_element_type=F32)
        y_ref[:, tp * tile:(tp + 1) * tile] = y_inter[:, tp * tile:(tp + 1) * tile] + y_intra


def _s5_operators(b_re, b_im, c_re, c_im, lam_re, lam_im, log_step):
    hp = lax.Precision.HIGHEST
    s, gb, w = S5_SUB, S5_GB, S5_GROUP_WIDTH
    n_g, n_p = lam_re.shape
    n_blk = n_g // gb
    lam = lax.complex(lam_re.astype(F32), lam_im.astype(F32))
    step = jnp.exp(log_step.astype(F32))[:, None]
    tau = jnp.arange(s + 1, dtype=F32)[:, None, None]
    lam_pow = jnp.exp((lam * step)[None] * tau)
    b_bar = ((lam_pow[1] - 1.0) / lam)[..., None] * lax.complex(b_re.astype(F32), b_im.astype(F32))
    cc = lax.complex(c_re.astype(F32), c_im.astype(F32))
    eye = jnp.eye(gb, dtype=F32)
    kern = jnp.einsum('gcp,tgp,gpd->tgcd', cc, lam_pow[:s], b_bar, precision=hp).real
    bd = jnp.einsum('tagcd,gh->atgdhc', kern.reshape(s, n_blk, gb, w, w), eye).reshape(n_blk, s, gb * w, gb * w)
    bd = jnp.concatenate([bd, jnp.zeros_like(bd[:, :1])], axis=1)
    two = jnp.arange(2)
    tile_lag = jnp.arange(s // 2 - 1, -1, -1)
    lag = 2 * tile_lag[:, None, None] + two[None, None, :] - two[None, :, None]
    mop = bd[:, jnp.where(lag < 0, s, lag)]
    mop = mop.transpose(0, 1, 2, 4, 3, 5).reshape(n_blk, s * gb * w, 2 * gb * w)
    ti = jnp.arange(s)
    gm = (lam_pow[s - 1 - ti][:, :, :, None] * b_bar[None]).reshape(s, n_blk, gb, n_p, w)
    gop = jnp.stack([jnp.einsum('sagpd,gh->asgdhp', part, eye) for part in (gm.real, gm.imag)], axis=4)
    gop = gop.reshape(n_blk, s * gb * w, 2 * gb * n_p)
    em = (cc[None] * lam_pow[1:s + 1][:, :, None, :]).reshape(s, n_blk, gb, w, n_p)
    cop = jnp.stack([jnp.einsum('tagcp,gh->agpthc', part, eye) for part in (em.real, -em.imag)], axis=1)
    cop = cop.reshape(n_blk, 2 * gb * n_p, s * gb * w)
    lam_s = lam_pow[s].reshape(n_blk, gb * n_p)
    lam_blk = jnp.stack([lam_s.real, lam_s.imag], axis=1)
    return mop.astype(BF16), gop.astype(BF16), cop.astype(BF16), lam_blk


def _glu_kernel(x_ref, y_ref, g_ref, d_ref, w_ref, b_ref, o_ref, y_scr):
    x = x_ref[...]
    d_model = x.shape[1]
    rows = y_ref.shape[1]
    for s in range(S5_SUB):
        for gb in range(y_ref.shape[0]):
            y_scr[gb, pl.ds(s, rows, stride=S5_SUB), :] = y_ref[gb, :, s * LANES:(s + 1) * LANES]
    y = jnp.concatenate([y_scr[gb] for gb in range(y_ref.shape[0])], axis=1)
    h = _rms(x, g_ref[...])
    act = jax.nn.gelu(y + d_ref[...] * h).astype(BF16)
    vg = jnp.dot(act, w_ref[...], preferred_element_type=F32) + b_ref[...]
    o_ref[...] = x + vg[:, :d_model] * jax.nn.sigmoid(vg[:, d_model:])


def _s5_call(x, bsz, seq, g_mix, b_re, b_im, c_re, c_im, lam_re, lam_im, log_step, d, glu_w, glu_b):
    t, dm = x.shape
    s = S5_SUB
    n_blk = dm // LANES
    n_sub = seq // s
    rows = t // s
    width = s * LANES
    state_w = 2 * S5_GB * S5_STATE
    n_bt = 4
    tm = 512
    assert seq % s == 0 and bsz % n_bt == 0 and t % tm == 0
    mop, gop, cop, lam_blk = _s5_operators(b_re, b_im, c_re, c_im, lam_re, lam_im, log_step)
    tok = lambda i: (i, 0)
    packed = lambda i: (0, i, 0)
    u = pl.pallas_call(
        _s5_pack_kernel,
        grid=(t // tm,),
        in_specs=[pl.BlockSpec((tm, dm), tok), _resident((1, dm))],
        out_specs=pl.BlockSpec((n_blk, tm // s, width), packed),
        out_shape=jax.ShapeDtypeStruct((n_blk, rows, width), BF16),
        scratch_shapes=[pltpu.VMEM((n_blk, tm, LANES), F32)],
        compiler_params=_params("parallel"),
        name="s5_pack",
    )(x, g_mix.reshape(1, dm))
    rt = n_bt * n_sub
    per_blk = lambda a, r: (a, 0, 0)
    once = pl.Buffered(1)
    y = pl.pallas_call(
        functools.partial(_s5_scan_kernel, n_bt=n_bt, n_sub=n_sub),
        grid=(n_blk, rows // rt),
        in_specs=[
            pl.BlockSpec((None, rt, width), lambda a, r: (a, r, 0)),
            pl.BlockSpec((None, width, 2 * LANES), per_blk, pipeline_mode=once),
            pl.BlockSpec((None, width, state_w), per_blk, pipeline_mode=once),
            pl.BlockSpec((None, state_w, width), per_blk, pipeline_mode=once),
            pl.BlockSpec((None, 2, state_w // 2), per_blk),
        ],
        out_specs=pl.BlockSpec((None, rt, width), lambda a, r: (a, r, 0)),
        out_shape=jax.ShapeDtypeStruct((n_blk, rows, width), F32),
        scratch_shapes=[pltpu.VMEM((state_w // LANES, rt, LANES), F32)] * 2,
        compiler_params=_params("arbitrary", "arbitrary"),
        name="s5_scan",
    )(u, mop, gop, cop, lam_blk)
    return pl.pallas_call(
        _glu_kernel,
        grid=(t // tm,),
        in_specs=[
            pl.BlockSpec((tm, dm), tok),
            pl.BlockSpec((n_blk, tm // s, width), packed),
            _resident((1, dm)), _resident((1, dm)), _resident((dm, 2 * dm)), _resident((1, 2 * dm)),
        ],
        out_specs=pl.BlockSpec((tm, dm), tok),
        out_shape=jax.ShapeDtypeStruct((t, dm), F32),
        scratch_shapes=[pltpu.VMEM((n_blk, tm, LANES), F32)],
        compiler_params=_params("parallel"),
        name="s5_glu",
    )(x, y, g_mix.reshape(1, dm), d.reshape(1, dm), glu_w.astype(BF16), glu_b.reshape(1, 2 * dm))


def _rope128(tile, cos, sin_hi, sin_lo):
    return tile * cos + pltpu.roll(tile, 32, 1) * sin_hi + pltpu.roll(tile, LANES - 32, 1) * sin_lo


def _dsa_proj_kernel(x_ref, g_ref, wq_ref, wqi_ref, wsm_ref, tab_ref, tabh_ref,
                     q_ref, qi_ref, k_ref, v_ref, ki_ref, wi_ref):
    h = _rms(x_ref[...], g_ref[...]).astype(BF16)
    cos, s_hi, s_lo = tab_ref[0], tab_ref[1], tab_ref[2]
    q = jnp.dot(h, wq_ref[...], preferred_element_type=F32)
    scale = ATTN_HEAD_DIM ** -0.5
    for j in range(q.shape[1] // LANES):
        sl = slice(j * LANES, (j + 1) * LANES)
        q_ref[:, sl] = (_rope128(q[:, sl], cos, s_hi, s_lo) * scale).astype(BF16)
    qi = jnp.dot(h, wqi_ref[...], preferred_element_type=F32)
    for j in range(qi.shape[1] // LANES):
        sl = slice(j * LANES, (j + 1) * LANES)
        qi_ref[:, sl] = _rope128(qi[:, sl], cos, s_hi, s_lo).astype(BF16)
    sm = jnp.dot(h, wsm_ref[...], preferred_element_type=F32)
    kv = _rope128(sm[:, :LANES], tabh_ref[0], tabh_ref[1], tabh_ref[2])
    kw = _rope128(sm[:, LANES:], tabh_ref[3], tabh_ref[1], tabh_ref[2])
    k_ref[...] = kv[:, :ATTN_HEAD_DIM].astype(BF16)
    v_ref[...] = kv[:, ATTN_HEAD_DIM:].astype(BF16)
    ki_ref[...] = kw[:, :IDX_DIM].astype(BF16)
    wi_ref[...] = kw


def _reduce_rows(x, op):
    slab = 8 * SUBLANES
    part = op(x.reshape(x.shape[0] // slab, slab, x.shape[1]), axis=0)
    return op(part, axis=0, keepdims=True)


def _dsa_attn_kernel(q_ref, qi_ref, wi_ref, k_ref, vt_ref, ki_ref, o_ref, key_scr, bias_scr, *, top_k, n_span):
    n = pl.program_id(1)
    per_span = pl.num_programs(1) // n_span
    span_keys = key_scr.shape[0] // n_span
    for i in range(n_span):
        @pl.when(n // per_span == i)
        def _(i=i):
            _dsa_attn_block(q_ref, qi_ref, wi_ref, k_ref, vt_ref, ki_ref, o_ref, key_scr, bias_scr,
                            top_k=top_k, seq=(i + 1) * span_keys)


def _dsa_attn_block(q_ref, qi_ref, wi_ref, k_ref, vt_ref, ki_ref, o_ref, key_full, bias_full, *, top_k, seq):
    n = pl.program_id(1)
    qb = key_full.shape[1]
    key_scr = key_full.at[0:seq, :]
    bias_scr = bias_full.at[0:seq, :]
    kib = ki_ref[0:seq, :]
    wi = wi_ref[...]
    hgi = IDX_HEADS // 2
    idx = jnp.zeros((seq, qb), F32)
    for g in range(IDX_HEADS // hgi):
        qis = jnp.concatenate([qi_ref[:, hh * IDX_DIM:(hh + 1) * IDX_DIM] for hh in range(g * hgi, (g + 1) * hgi)],
                              axis=0)
        sc = lax.dot_general(kib, qis, _NT, preferred_element_type=F32)
        for j in range(hgi):
            hh = g * hgi + j
            idx = idx + jnp.maximum(sc[:, j * qb:(j + 1) * qb], 0.0) * wi[hh:hh + 1, :]
    t_pos = n * qb + lax.broadcasted_iota(jnp.int32, (1, qb), 1)
    s_pos = lax.broadcasted_iota(jnp.int32, (seq, 1), 0)
    causal = s_pos <= t_pos
    idx = jnp.where(causal, idx, -jnp.inf)
    bits = lax.bitcast_convert_type(idx, jnp.int32)
    key_scr[...] = jnp.where(bits < 0, bits ^ jnp.int32(0x7FFFFFFF), bits)

    def count(pred):
        return _reduce_rows(jnp.where(pred, 1.0, 0.0), jnp.sum)

    int_min = jnp.int32(-2 ** 31)
    kf = float(top_k)
    thr0 = jnp.where(count(key_scr[...] >= 0) >= kf, jnp.int32(0), int_min)

    def thr_step(i, thr):
        cand = thr + lax.shift_left(jnp.int32(1), jnp.int32(30) - i)
        return jnp.where(count(key_scr[...] >= cand) >= kf, cand, thr)

    thr = lax.fori_loop(0, 31, thr_step, thr0)
    keys = key_scr[...]
    n_ge = count(keys >= thr)

    def tie_cut():
        need = kf - count(key_scr[...] > thr)
        n_bits = (seq - 1).bit_length()

        def cut_step(i, cut):
            cand = cut + lax.shift_left(jnp.int32(1), jnp.int32(n_bits - 1) - i)
            return jnp.where(count((key_scr[...] == thr) & (s_pos < cand)) < need, cand, cut)

        return lax.fori_loop(0, n_bits, cut_step, jnp.zeros((1, qb), jnp.int32))

    cut = lax.cond(jnp.max(n_ge) > kf, tie_cut, lambda: jnp.full((1, qb), seq, jnp.int32))
    sel = (keys > thr) | ((keys == thr) & (s_pos <= cut))
    bias_scr[...] = jnp.where(sel & causal, 0.0, -jnp.inf)

    kb = k_ref[0:seq, :]
    vt = vt_ref[:, 0:seq]
    hg = DSA_HEAD_GROUP
    dh = ATTN_HEAD_DIM
    for g in range(ATTN_HEADS // hg):
        qs = jnp.concatenate([q_ref[:, (g * hg + j) * dh:(g * hg + j + 1) * dh] for j in range(hg)], axis=0)
        lg = lax.dot_general(kb, qs, _NT, preferred_element_type=F32)
        lg = lg + jnp.concatenate([bias_scr[...]] * hg, axis=1)
        p = jnp.exp(lg - _reduce_rows(lg, jnp.max))
        den = _reduce_rows(p, jnp.sum)
        og = jnp.dot(vt, p.astype(BF16), preferred_element_type=F32) / den
        for j in range(0, hg, 2):
            pair = jnp.concatenate([og[:, j * qb:(j + 1) * qb], og[:, (j + 1) * qb:(j + 2) * qb]], axis=0)
            lo = (g * hg + j) * dh
            o_ref[:, lo:lo + 2 * dh] = pair.T.astype(BF16)


def _proj_res_kernel(a_ref, w_ref, x_ref, o_ref):
    o_ref[...] = x_ref[...] + jnp.dot(a_ref[...], w_ref[...], preferred_element_type=F32)


def _rope_tables(seq):
    half = ATTN_HEAD_DIM // 2
    inv = ROPE_THETA ** (-jnp.arange(half, dtype=F32) / half)
    ang = jnp.arange(seq, dtype=F32)[:, None] * inv
    cos, sin = jnp.cos(ang), jnp.sin(ang)
    zero, one = jnp.zeros_like(cos), jnp.ones_like(cos)
    cos64 = jnp.concatenate([cos, cos], axis=1)
    hi64 = jnp.concatenate([zero, sin], axis=1)
    lo64 = jnp.concatenate([-sin, zero], axis=1)
    full = jnp.stack([jnp.tile(t, (1, 2)) for t in (cos64, hi64, lo64)])
    ident = jnp.concatenate([one, one], axis=1)
    zero64 = jnp.concatenate([zero, zero], axis=1)
    wscale = ident * (IDX_HEADS ** -0.5 * IDX_DIM ** -0.5)
    halfs = jnp.stack([jnp.concatenate([cos64, ident], axis=1), jnp.concatenate([hi64, zero64], axis=1),
                       jnp.concatenate([lo64, zero64], axis=1), jnp.concatenate([cos64, wscale], axis=1)])
    return full, halfs


def _dsa_call(x, bsz, seq, g_mix, w_in, w_out):
    t, dm = x.shape
    dq = ATTN_HEADS * ATTN_HEAD_DIM
    dqi = IDX_HEADS * IDX_DIM
    o_k, o_v, o_qi = dq, dq + ATTN_HEAD_DIM, dq + 2 * ATTN_HEAD_DIM
    o_ki = o_qi + dqi
    o_wi = o_ki + IDX_DIM
    wq = w_in[:, :o_k].astype(BF16)
    wqi = w_in[:, o_qi:o_ki].astype(BF16)
    wsm = jnp.concatenate([w_in[:, o_k:o_qi], w_in[:, o_ki:],
                           jnp.zeros((dm, LANES - IDX_DIM - IDX_HEADS), w_in.dtype)], axis=1).astype(BF16)
    full, halfs = _rope_tables(seq)
    tm = 256
    nt = seq // tm
    row = lambda i: (i, 0)
    tab = lambda i: (0, i % nt, 0)
    q, qi, k, v, ki, wi = pl.pallas_call(
        _dsa_proj_kernel,
        grid=(t // tm,),
        in_specs=[
            pl.BlockSpec((tm, dm), row), _resident((1, dm)), _resident(wq.shape), _resident(wqi.shape),
            _resident(wsm.shape),
            pl.BlockSpec((3, tm, LANES), tab), pl.BlockSpec((4, tm, LANES), tab),
        ],
        out_specs=[
            pl.BlockSpec((tm, dq), row), pl.BlockSpec((tm, dqi), row),
            pl.BlockSpec((tm, ATTN_HEAD_DIM), row), pl.BlockSpec((tm, ATTN_HEAD_DIM), row),
            pl.BlockSpec((tm, IDX_DIM), row), pl.BlockSpec((tm, LANES), row),
        ],
        out_shape=[
            jax.ShapeDtypeStruct((t, dq), BF16), jax.ShapeDtypeStruct((t, dqi), BF16),
            jax.ShapeDtypeStruct((t, ATTN_HEAD_DIM), BF16), jax.ShapeDtypeStruct((t, ATTN_HEAD_DIM), BF16),
            jax.ShapeDtypeStruct((t, IDX_DIM), BF16), jax.ShapeDtypeStruct((t, LANES), F32),
        ],
        compiler_params=_params("parallel"),
        name="dsa_proj",
    )(x, g_mix.reshape(1, dm), wq, wqi, wsm, full, halfs)

    qb = QUERY_BLOCK
    nb = seq // qb
    top_k = min(TOPK_MAX, seq // 4)
    blk = lambda b, n: (b * nb + n, 0)
    per_b = lambda b, n: (b, 0, 0)
    wi_t = wi[:, IDX_DIM:IDX_DIM + IDX_HEADS].reshape(bsz, seq, IDX_HEADS).transpose(0, 2, 1)
    v_t = v.reshape(bsz, seq, ATTN_HEAD_DIM).transpose(0, 2, 1)
    o = pl.pallas_call(
        functools.partial(_dsa_attn_kernel, top_k=top_k, n_span=math.gcd(nb, 8)),
        grid=(bsz, nb),
        in_specs=[
            pl.BlockSpec((qb, dq), blk), pl.BlockSpec((qb, dqi), blk),
            pl.BlockSpec((None, IDX_HEADS, qb), lambda b, n: (b, 0, n)),
            pl.BlockSpec((None, seq, ATTN_HEAD_DIM), per_b), pl.BlockSpec((None, ATTN_HEAD_DIM, seq), per_b),
            pl.BlockSpec((None, seq, IDX_DIM), per_b),
        ],
        out_specs=pl.BlockSpec((qb, dq), blk),
        out_shape=jax.ShapeDtypeStruct((t, dq), BF16),
        scratch_shapes=[pltpu.VMEM((seq, qb), jnp.int32), pltpu.VMEM((seq, qb), F32)],
        compiler_params=_params("parallel", "arbitrary"),
        name="dsa_attn",
    )(q, qi, wi_t, k.reshape(bsz, seq, -1), v_t, ki.reshape(bsz, seq, -1))

    tmo = 512
    return pl.pallas_call(
        _proj_res_kernel,
        grid=(t // tmo,),
        in_specs=[pl.BlockSpec((tmo, dq), row), _resident((dq, dm)), pl.BlockSpec((tmo, dm), row)],
        out_specs=pl.BlockSpec((tmo, dm), row),
        out_shape=jax.ShapeDtypeStruct((t, dm), F32),
        compiler_params=_params("parallel"),
        name="dsa_out",
    )(o, w_out.astype(BF16), x)


def kernel(x, ffn1_norm, ffn1_w_in, ffn1_w_out, mix_norm, ffn2_norm, ffn2_w_in, ffn2_w_out, ssd_in_proj, ssd_conv_w, ssd_conv_b, ssd_dt_bias, ssd_a_log, ssd_d, ssd_gate_norm, ssd_out_proj, s5_b_re, s5_b_im, s5_c_re, s5_c_im, s5_lam_re, s5_lam_im, s5_log_step, s5_d, s5_glu_w, s5_glu_b, dsa_in_proj, dsa_out_proj, final_norm):
    bsz, seq, dm = x.shape
    depth = ffn1_norm.shape[0]
    xt = x.reshape(bsz * seq, dm)
    for i in range(depth):
        j, kind = divmod(i, N_MIXERS)
        xt = _ffn_call(xt, ffn1_norm[i], ffn1_w_in[i].astype(BF16), ffn1_w_out[i].astype(BF16))
        if kind == 0:
            xt = _ssd_call(xt, bsz, seq, mix_norm[i], ssd_in_proj[j], ssd_conv_w[j], ssd_conv_b[j],
                           ssd_dt_bias[j], ssd_a_log[j], ssd_d[j], ssd_gate_norm[j], ssd_out_proj[j])
        elif kind == 1:
            xt = _s5_call(xt, bsz, seq, mix_norm[i], s5_b_re[j], s5_b_im[j], s5_c_re[j], s5_c_im[j],
                          s5_lam_re[j], s5_lam_im[j], s5_log_step[j], s5_d[j], s5_glu_w[j], s5_glu_b[j])
        else:
            xt = _dsa_call(xt, bsz, seq, mix_norm[i], dsa_in_proj[j], dsa_out_proj[j])
        post = final_norm if i == depth - 1 else None
        xt = _ffn_call(xt, ffn2_norm[i], ffn2_w_in[i].astype(BF16), ffn2_w_out[i].astype(BF16), post)
    return xt.reshape(bsz, seq, dm)
```

```python
import functools
import math

import jax
import jax.numpy as jnp
from jax import lax
from jax.experimental import pallas as pl
from jax.experimental.pallas import tpu as pltpu

F32 = jnp.float32
BF16 = jnp.bfloat16

NORM_EPS = 1e-6
N_MIXERS = 3

SSD_HEAD_DIM = 64
SSD_GROUPS = 8
SSD_HEADS_PER_GROUP = 4
SSD_HEADS = SSD_GROUPS * SSD_HEADS_PER_GROUP
SSD_STATE = 128
SSD_INNER = SSD_HEADS * SSD_HEAD_DIM
SSD_CONV = 4
SSD_CHUNK_MAX = 256
SSD_GROUP_WIDTH = SSD_HEADS_PER_GROUP * SSD_HEAD_DIM

S5_GROUP_WIDTH = 16
S5_STATE = 64
S5_SUB = 16

ATTN_HEADS = 16
ATTN_HEAD_DIM = 64
IDX_HEADS = 8
IDX_DIM = 64
TOPK_MAX = 256
QUERY_BLOCK = 128
ROPE_THETA = 10000.0
DSA_HEAD_GROUP = 4

LANES = 128
SUBLANES = 8
VMEM_LIMIT_BYTES = 56 * 1024 * 1024

_NT = (((1,), (1,)), ((), ()))


def _params(*semantics):
    return pltpu.CompilerParams(dimension_semantics=semantics, vmem_limit_bytes=VMEM_LIMIT_BYTES)


def _rms(x, g):
    ms = jnp.mean(x * x, axis=-1, keepdims=True)
    return x * lax.rsqrt(ms + NORM_EPS) * g


def _silu(x):
    return x * jax.nn.sigmoid(x)


def _softplus(x):
    return jnp.maximum(x, 0.0) + jnp.log1p(jnp.exp(-jnp.abs(x)))


def _resident(shape):
    zeros = (0,) * len(shape)
    return pl.BlockSpec(shape, lambda *_: zeros, pipeline_mode=pl.Buffered(1))


FFN_CHUNK = 256


def _ffn_kernel(x_ref, g_ref, win_ref, wout_ref, pg_ref, o_ref, *, post_norm):
    x = x_ref[...]
    h = _rms(x, g_ref[...]).astype(BF16)
    f = wout_ref.shape[0]
    acc = jnp.zeros(x.shape, F32)
    for j in range(f // FFN_CHUNK):
        lo = j * FFN_CHUNK
        gate = jnp.dot(h, win_ref[:, lo:lo + FFN_CHUNK], preferred_element_type=F32)
        up = jnp.dot(h, win_ref[:, f + lo:f + lo + FFN_CHUNK], preferred_element_type=F32)
        act = (_silu(gate) * up).astype(BF16)
        acc = acc + jnp.dot(act, wout_ref[lo:lo + FFN_CHUNK, :], preferred_element_type=F32)
    y = x + 0.5 * acc
    if post_norm:
        y = _rms(y, pg_ref[...])
    o_ref[...] = y


def _ffn_call(x, g, w_in, w_out, post_g=None):
    t, d = x.shape
    f = w_out.shape[0]
    tm = 1024
    assert t % tm == 0 and f % FFN_CHUNK == 0
    post_norm = post_g is not None
    pg = post_g if post_norm else g
    return pl.pallas_call(
        functools.partial(_ffn_kernel, post_norm=post_norm),
        grid=(t // tm,),
        in_specs=[
            pl.BlockSpec((tm, d), lambda i: (i, 0)),
            _resident((1, d)), _resident(w_in.shape), _resident(w_out.shape), _resident((1, d)),
        ],
        out_specs=pl.BlockSpec((tm, d), lambda i: (i, 0)),
        out_shape=jax.ShapeDtypeStruct((t, d), F32),
        compiler_params=_params("parallel"),
        name="ffn",
    )(x, g.reshape(1, d), w_in, w_out, pg.reshape(1, d))


def _split3(a):
    hi = a.astype(BF16)
    r1 = a - hi.astype(F32)
    mid = r1.astype(BF16)
    lo = (r1 - mid.astype(F32)).astype(BF16)
    return hi, mid, lo


def _ssd_kernel(x_ref, gmix_ref, wz_ref, wxbc_ref, wdt_ref, wdtt_ref, convw_ref, convb_ref,
                dtbc_ref, dtbr_ref, ac_ref, ar_ref, dvec_ref, gn_ref, wout_ref, selg_ref, selh_ref,
                o_ref,
                raw_scr, carry_scr, xs_scr, bm_scr, cm_scr, z_scr, yn_scr, dtc_scr, dtr_scr, state_scr,
                *, q, n_ch):
    pad = SUBLANES
    rows = n_ch * q
    gw = SSD_GROUP_WIDTH
    hd_dim = SSD_HEAD_DIM
    n_state = SSD_STATE
    hpg = SSD_HEADS_PER_GROUP

    @pl.when(pl.program_id(1) == 0)
    def _():
        state_scr[...] = jnp.zeros_like(state_scr)
        carry_scr[...] = jnp.zeros_like(carry_scr)

    x = x_ref[...]
    h = _rms(x, gmix_ref[...]).astype(BF16)
    z_scr[...] = jnp.dot(h, wz_ref[...], preferred_element_type=F32)

    cw = raw_scr.shape[1]
    n_x = SSD_INNER // cw
    n_b = SSD_GROUPS * n_state // cw
    for j in range(wxbc_ref.shape[1] // cw):
        sl = slice(j * cw, (j + 1) * cw)
        raw = raw_scr
        raw[0:pad, :] = carry_scr[:, sl]
        raw[pad:pad + rows, :] = jnp.dot(h, wxbc_ref[:, sl], preferred_element_type=F32)
        carry_scr[:, sl] = raw[rows:rows + pad, :]
        w = convw_ref[:, sl]
        acc = raw[pad - 3:pad - 3 + rows, :] * w[0:1]
        acc = acc + raw[pad - 2:pad - 2 + rows, :] * w[1:2]
        acc = acc + raw[pad - 1:pad - 1 + rows, :] * w[2:3]
        acc = acc + raw[pad:pad + rows, :] * w[3:4]
        act = _silu(acc + convb_ref[:, sl])
        if j < n_x:
            xs_scr[:, sl] = act
        elif j < n_x + n_b:
            bm_scr[:, (j - n_x) * cw:(j - n_x + 1) * cw] = act
        else:
            cm_scr[:, (j - n_x - n_b) * cw:(j - n_x - n_b + 1) * cw] = act

    dtc_scr[...] = _softplus(jnp.dot(h, wdt_ref[...], preferred_element_type=F32) + dtbc_ref[...])
    dt_rows = _softplus(lax.dot_general(wdtt_ref[...], h, _NT, preferred_element_type=F32) + dtbr_ref[...])
    for ci in range(n_ch):
        dtr_scr[ci] = dt_rows[:, ci * q:(ci + 1) * q]
    row = lax.broadcasted_iota(jnp.int32, (q, q), 0)
    col = lax.broadcasted_iota(jnp.int32, (q, q), 1)
    causal = col <= row
    tri_low = jnp.where(causal, 1.0, 0.0).astype(BF16)
    tri_up = jnp.where(row <= col, 1.0, 0.0).astype(BF16)

    def expand(a, sel):
        return sum(jnp.dot(p, sel, preferred_element_type=F32) for p in _split3(a))

    def chunk(ci, carry):
        r = pl.ds(pl.multiple_of(ci * q, q), q)
        dt_r = dtr_scr[ci]
        a_c = dtc_scr[r, :] * ac_ref[...]
        a_r = dt_r * ar_ref[...]
        cs_c = sum(jnp.dot(tri_low, p, preferred_element_type=F32) for p in _split3(a_c))
        cs_r = sum(jnp.dot(p, tri_up, preferred_element_type=F32) for p in _split3(a_r))
        e_c = jnp.exp(cs_c)
        cs_end = cs_r[:, q - 1:q]
        f_r = dt_r * jnp.exp(cs_end - cs_r)
        e_end = jnp.exp(cs_end)
        for g in range(SSD_GROUPS):
            bg = bm_scr[r, g * n_state:(g + 1) * n_state].astype(BF16)
            cg = cm_scr[r, g * n_state:(g + 1) * n_state].astype(BF16)
            cb = lax.dot_general(cg, bg, _NT, preferred_element_type=F32)
            xs_g = xs_scr[r, g * gw:(g + 1) * gw]
            xs_gb = xs_g.astype(BF16)
            xs_gt = xs_g.T
            st_g = state_scr[g]
            y_off = lax.dot_general(cg, st_g.astype(BF16), _NT, preferred_element_type=F32)
            cs_cols = expand(cs_c, selh_ref[g])
            ys, ws, decs = [], [], []
            for k in range(hpg):
                hd = g * hpg + k
                cs_col = cs_cols[:, k * LANES:(k + 1) * LANES]
                seg = jnp.concatenate([cs_col] * (q // LANES), axis=1) - cs_r[hd:hd + 1, :]
                decay = jnp.exp(jnp.where(causal, seg, -jnp.inf))
                m = (cb * decay * dt_r[hd:hd + 1, :]).astype(BF16)
                ys.append(jnp.dot(m, xs_gb[:, k * hd_dim:(k + 1) * hd_dim], preferred_element_type=F32))
                ws.append(xs_gt[k * hd_dim:(k + 1) * hd_dim, :] * f_r[hd:hd + 1, :])
                decs.append(jnp.broadcast_to(e_end[hd:hd + 1, :], (hd_dim, n_state)))
            wt = jnp.concatenate(ws, axis=0)
            upd = jnp.dot(wt.astype(BF16), bg, preferred_element_type=F32)
            state_scr[g] = st_g * jnp.concatenate(decs, axis=0) + upd
            y_g = jnp.concatenate(ys, axis=1) + y_off * expand(e_c, selg_ref[g])
            y_g = y_g + dvec_ref[:, g * gw:(g + 1) * gw] * xs_g
            yg = y_g * _silu(z_scr[r, g * gw:(g + 1) * gw])
            yg = yg * lax.rsqrt(jnp.mean(yg * yg, axis=-1, keepdims=True) + NORM_EPS)
            yn_scr[r, g * gw:(g + 1) * gw] = (yg * gn_ref[:, g * gw:(g + 1) * gw]).astype(BF16)
        return carry

    lax.fori_loop(0, n_ch, chunk, 0)
    o_ref[...] = x + jnp.dot(yn_scr[...], wout_ref[...], preferred_element_type=F32)


def _ssd_call(x, bsz, seq, g_mix, in_proj, conv_w, conv_b, dt_bias, a_log, d, gate_norm, out_proj):
    t, dm = x.shape
    q = math.gcd(seq, SSD_CHUNK_MAX)
    nc = seq // q
    conv_dim = SSD_INNER + 2 * SSD_GROUPS * SSD_STATE
    wz = in_proj[:, :SSD_INNER].astype(BF16)
    wxbc = in_proj[:, SSD_INNER:SSD_INNER + conv_dim].astype(BF16)
    wdt = in_proj[:, SSD_INNER + conv_dim:]
    wdt_c = jnp.pad(wdt, ((0, 0), (0, LANES - SSD_HEADS))).astype(BF16)
    wdt_r = wdt.T.astype(BF16)
    a = -jnp.exp(a_log.astype(F32))
    lane_pad = (0, LANES - SSD_HEADS)
    head_of = jnp.arange(LANES)
    grp_heads = jnp.arange(SSD_GROUPS) * SSD_HEADS_PER_GROUP
    args = (
        x, g_mix.reshape(1, dm), wz, wxbc, wdt_c, wdt_r, conv_w, conv_b.reshape(1, conv_dim),
        jnp.pad(dt_bias, lane_pad).reshape(1, LANES), dt_bias.reshape(SSD_HEADS, 1),
        jnp.pad(a, lane_pad).reshape(1, LANES), a.reshape(SSD_HEADS, 1),
        jnp.repeat(d, SSD_HEAD_DIM).reshape(1, SSD_INNER), gate_norm.reshape(1, SSD_INNER),
        out_proj.astype(BF16),
        (head_of[:, None] == (grp_heads[:, None, None] + jnp.arange(SSD_GROUP_WIDTH) // SSD_HEAD_DIM)).astype(BF16),
        (head_of[:, None] == (grp_heads[:, None, None] + jnp.arange(SSD_HEADS_PER_GROUP * LANES) // LANES)).astype(BF16),
    )
    n_ch = math.gcd(nc, 2)
    steps = nc // n_ch
    rows = n_ch * q
    x_spec = pl.BlockSpec((rows, dm), lambda b, c: (b * steps + c, 0))
    in_specs = [x_spec] + [_resident(a_.shape) for a_ in args[1:]]
    return pl.pallas_call(
        functools.partial(_ssd_kernel, q=q, n_ch=n_ch),
        grid=(bsz, steps),
        in_specs=in_specs,
        out_specs=x_spec,
        out_shape=jax.ShapeDtypeStruct((t, dm), F32),
        scratch_shapes=[
            pltpu.VMEM((rows + 2 * SUBLANES, 4 * LANES), F32),
            pltpu.VMEM((SUBLANES, conv_dim), F32),
            pltpu.VMEM((rows, SSD_INNER), F32),
            pltpu.VMEM((rows, SSD_GROUPS * SSD_STATE), F32),
            pltpu.VMEM((rows, SSD_GROUPS * SSD_STATE), F32),
            pltpu.VMEM((rows, SSD_INNER), F32),
            pltpu.VMEM((rows, SSD_INNER), BF16),
            pltpu.VMEM((rows, LANES), F32),
            pltpu.VMEM((n_ch, SSD_HEADS, q), F32),
            pltpu.VMEM((SSD_GROUPS, SSD_GROUP_WIDTH, SSD_STATE), F32),
        ],
        compiler_params=_params("arbitrary", "arbitrary"),
        name="ssd_mixer",
    )(*args)


S5_GB = LANES // S5_GROUP_WIDTH


def _s5_pack_kernel(x_ref, g_ref, o_ref, h_scr):
    h = _rms(x_ref[...], g_ref[...])
    rows = o_ref.shape[1]
    for gb in range(o_ref.shape[0]):
        h_scr[gb] = h[:, gb * LANES:(gb + 1) * LANES]
    for s in range(S5_SUB):
        for gb in range(o_ref.shape[0]):
            piece = h_scr[gb, pl.ds(s, rows, stride=S5_SUB), :]
            o_ref[gb, :, s * LANES:(s + 1) * LANES] = piece.astype(o_ref.dtype)


def _s5_scan_kernel(u_ref, mop_ref, gop_ref, cop_ref, lam_ref, y_ref, g_scr, hp_scr, *, n_bt, n_sub):
    n_half = g_scr.shape[0] // 2
    tile = 2 * LANES
    u = u_ref[...]
    g_in = jnp.dot(u, gop_ref[...], preferred_element_type=F32)
    for c in range(2 * n_half):
        g_scr[c] = g_in[:, c * LANES:(c + 1) * LANES]
    lam = lam_ref[...]
    lr = [jnp.broadcast_to(lam[0:1, c * LANES:(c + 1) * LANES], (n_bt, LANES)) for c in range(n_half)]
    li = [jnp.broadcast_to(lam[1:2, c * LANES:(c + 1) * LANES], (n_bt, LANES)) for c in range(n_half)]
    hr = [jnp.zeros((n_bt, LANES), F32)] * n_half
    hi = [jnp.zeros((n_bt, LANES), F32)] * n_half
    for j in range(n_sub):
        rows = pl.ds(j, n_bt, stride=n_sub)
        for c in range(n_half):
            hp_scr[c, rows, :] = hr[c]
            hp_scr[n_half + c, rows, :] = hi[c]
            gr, gi = g_scr[c, rows, :], g_scr[n_half + c, rows, :]
            hr[c], hi[c] = lr[c] * hr[c] - li[c] * hi[c] + gr, lr[c] * hi[c] + li[c] * hr[c] + gi
    h_in = jnp.concatenate([hp_scr[c] for c in range(2 * n_half)], axis=1).astype(BF16)
    y_inter = jnp.dot(h_in, cop_ref[...], preferred_element_type=F32)
    n_t = u.shape[1] // tile
    for tp in range(n_t):
        k_hi = (tp + 1) * tile
        y_intra = jnp.dot(u[:, :k_hi], mop_ref[(n_t - 1 - tp) * tile:, :], preferred_element_type=F32)
        y_ref[:, tp * tile:(tp + 1) * tile] = y_inter[:, tp * tile:(tp + 1) * tile] + y_intra


def _s5_operators(b_re, b_im, c_re, c_im, lam_re, lam_im, log_step):
    hp = lax.Precision.HIGHEST
    s, gb, w = S5_SUB, S5_GB, S5_GROUP_WIDTH
    n_g, n_p = lam_re.shape
    n_blk = n_g // gb
    lam = lax.complex(lam_re.astype(F32), lam_im.astype(F32))
    step = jnp.exp(log_step.astype(F32))[:, None]
    tau = jnp.arange(s + 1, dtype=F32)[:, None, None]
    lam_pow = jnp.exp((lam * step)[None] * tau)
    b_bar = ((lam_pow[1] - 1.0) / lam)[..., None] * lax.complex(b_re.astype(F32), b_im.astype(F32))
    cc = lax.complex(c_re.astype(F32), c_im.astype(F32))
    eye = jnp.eye(gb, dtype=F32)
    kern = jnp.einsum('gcp,tgp,gpd->tgcd', cc, lam_pow[:s], b_bar, precision=hp).real
    bd = jnp.einsum('tagcd,gh->atgdhc', kern.reshape(s, n_blk, gb, w, w), eye).reshape(n_blk, s, gb * w, gb * w)
    bd = jnp.concatenate([bd, jnp.zeros_like(bd[:, :1])], axis=1)
    two = jnp.arange(2)
    tile_lag = jnp.arange(s // 2 - 1, -1, -1)
    lag = 2 * tile_lag[:, None, None] + two[None, None, :] - two[None, :, None]
    mop = bd[:, jnp.where(lag < 0, s, lag)]
    mop = mop.transpose(0, 1, 2, 4, 3, 5).reshape(n_blk, s * gb * w, 2 * gb * w)
    ti = jnp.arange(s)
    gm = (lam_pow[s - 1 - ti][:, :, :, None] * b_bar[None]).reshape(s, n_blk, gb, n_p, w)
    gop = jnp.stack([jnp.einsum('sagpd,gh->asgdhp', part, eye) for part in (gm.real, gm.imag)], axis=4)
    gop = gop.reshape(n_blk, s * gb * w, 2 * gb * n_p)
    em = (cc[None] * lam_pow[1:s + 1][:, :, None, :]).reshape(s, n_blk, gb, w, n_p)
    cop = jnp.stack([jnp.einsum('tagcp,gh->agpthc', part, eye) for part in (em.real, -em.imag)], axis=1)
    cop = cop.reshape(n_blk, 2 * gb * n_p, s * gb * w)
    lam_s = lam_pow[s].reshape(n_blk, gb * n_p)
    lam_blk = jnp.stack([lam_s.real, lam_s.imag], axis=1)
    return mop.astype(BF16), gop.astype(BF16), cop.astype(BF16), lam_blk


def _glu_kernel(x_ref, y_ref, g_ref, d_ref, w_ref, b_ref, o_ref, y_scr):
    x = x_ref[...]
    d_model = x.shape[1]
    rows = y_ref.shape[1]
    for s in range(S5_SUB):
        for gb in range(y_ref.shape[0]):
            y_scr[gb, pl.ds(s, rows, stride=S5_SUB), :] = y_ref[gb, :, s * LANES:(s + 1) * LANES]
    y = jnp.concatenate([y_scr[gb] for gb in range(y_ref.shape[0])], axis=1)
    h = _rms(x, g_ref[...])
    act = jax.nn.gelu(y + d_ref[...] * h).astype(BF16)
    vg = jnp.dot(act, w_ref[...], preferred_element_type=F32) + b_ref[...]
    o_ref[...] = x + vg[:, :d_model] * jax.nn.sigmoid(vg[:, d_model:])


def _s5_call(x, bsz, seq, g_mix, b_re, b_im, c_re, c_im, lam_re, lam_im, log_step, d, glu_w, glu_b):
    t, dm = x.shape
    s = S5_SUB
    n_blk = dm // LANES
    n_sub = seq // s
    rows = t // s
    width = s * LANES
    state_w = 2 * S5_GB * S5_STATE
    n_bt = 4
    tm = 512
    assert seq % s == 0 and bsz % n_bt == 0 and t % tm == 0
    mop, gop, cop, lam_blk = _s5_operators(b_re, b_im, c_re, c_im, lam_re, lam_im, log_step)
    tok = lambda i: (i, 0)
    packed = lambda i: (0, i, 0)
    u = pl.pallas_call(
        _s5_pack_kernel,
        grid=(t // tm,),
        in_specs=[pl.BlockSpec((tm, dm), tok), _resident((1, dm))],
        out_specs=pl.BlockSpec((n_blk, tm // s, width), packed),
        out_shape=jax.ShapeDtypeStruct((n_blk, rows, width), BF16),
        scratch_shapes=[pltpu.VMEM((n_blk, tm, LANES), F32)],
        compiler_params=_params("parallel"),
        name="s5_pack",
    )(x, g_mix.reshape(1, dm))
    rt = n_bt * n_sub
    per_blk = lambda a, r: (a, 0, 0)
    once = pl.Buffered(1)
    y = pl.pallas_call(
        functools.partial(_s5_scan_kernel, n_bt=n_bt, n_sub=n_sub),
        grid=(n_blk, rows // rt),
        in_specs=[
            pl.BlockSpec((None, rt, width), lambda a, r: (a, r, 0)),
            pl.BlockSpec((None, width, 2 * LANES), per_blk, pipeline_mode=once),
            pl.BlockSpec((None, width, state_w), per_blk, pipeline_mode=once),
            pl.BlockSpec((None, state_w, width), per_blk, pipeline_mode=once),
            pl.BlockSpec((None, 2, state_w // 2), per_blk),
        ],
        out_specs=pl.BlockSpec((None, rt, width), lambda a, r: (a, r, 0)),
        out_shape=jax.ShapeDtypeStruct((n_blk, rows, width), F32),
        scratch_shapes=[pltpu.VMEM((state_w // LANES, rt, LANES), F32)] * 2,
        compiler_params=_params("arbitrary", "arbitrary"),
        name="s5_scan",
    )(u, mop, gop, cop, lam_blk)
    return pl.pallas_call(
        _glu_kernel,
        grid=(t // tm,),
        in_specs=[
            pl.BlockSpec((tm, dm), tok),
            pl.BlockSpec((n_blk, tm // s, width), packed),
            _resident((1, dm)), _resident((1, dm)), _resident((dm, 2 * dm)), _resident((1, 2 * dm)),
        ],
        out_specs=pl.BlockSpec((tm, dm), tok),
        out_shape=jax.ShapeDtypeStruct((t, dm), F32),
        scratch_shapes=[pltpu.VMEM((n_blk, tm, LANES), F32)],
        compiler_params=_params("parallel"),
        name="s5_glu",
    )(x, y, g_mix.reshape(1, dm), d.reshape(1, dm), glu_w.astype(BF16), glu_b.reshape(1, 2 * dm))


def _rope128(tile, cos, sin_hi, sin_lo):
    return tile * cos + pltpu.roll(tile, 32, 1) * sin_hi + pltpu.roll(tile, LANES - 32, 1) * sin_lo


def _dsa_proj_kernel(x_ref, g_ref, wq_ref, wqi_ref, wsm_ref, tab_ref, tabh_ref,
                     q_ref, qi_ref, k_ref, v_ref, ki_ref, wi_ref):
    h = _rms(x_ref[...], g_ref[...]).astype(BF16)
    cos, s_hi, s_lo = tab_ref[0], tab_ref[1], tab_ref[2]
    q = jnp.dot(h, wq_ref[...], preferred_element_type=F32)
    scale = ATTN_HEAD_DIM ** -0.5
    for j in range(q.shape[1] // LANES):
        sl = slice(j * LANES, (j + 1) * LANES)
        q_ref[:, sl] = (_rope128(q[:, sl], cos, s_hi, s_lo) * scale).astype(BF16)
    qi = jnp.dot(h, wqi_ref[...], preferred_element_type=F32)
    for j in range(qi.shape[1] // LANES):
        sl = slice(j * LANES, (j + 1) * LANES)
        qi_ref[:, sl] = _rope128(qi[:, sl], cos, s_hi, s_lo).astype(BF16)
    sm = jnp.dot(h, wsm_ref[...], preferred_element_type=F32)
    kv = _rope128(sm[:, :LANES], tabh_ref[0], tabh_ref[1], tabh_ref[2])
    kw = _rope128(sm[:, LANES:], tabh_ref[3], tabh_ref[1], tabh_ref[2])
    k_ref[...] = kv[:, :ATTN_HEAD_DIM].astype(BF16)
    v_ref[...] = kv[:, ATTN_HEAD_DIM:].astype(BF16)
    ki_ref[...] = kw[:, :IDX_DIM].astype(BF16)
    wi_ref[...] = kw


def _reduce_rows(x, op):
    slab = 8 * SUBLANES
    part = op(x.reshape(x.shape[0] // slab, slab, x.shape[1]), axis=0)
    return op(part, axis=0, keepdims=True)


def _dsa_attn_kernel(q_ref, qi_ref, wi_ref, k_ref, vt_ref, ki_ref, o_ref,
                     key_scr, bias_scr, qs_scr, qis_scr, og_scr, *, top_k, n_span):
    n = pl.program_id(1)
    per_span = pl.num_programs(1) // n_span
    span_keys = key_scr.shape[0] // n_span
    hg, dh = DSA_HEAD_GROUP, ATTN_HEAD_DIM
    for g in range(qs_scr.shape[0]):
        qs_scr[g] = jnp.concatenate([q_ref[:, (g * hg + j) * dh:(g * hg + j + 1) * dh] for j in range(hg)], axis=0)
    for g in range(qis_scr.shape[0]):
        qis_scr[g] = jnp.concatenate(
            [qi_ref[:, (g * hg + j) * IDX_DIM:(g * hg + j + 1) * IDX_DIM] for j in range(hg)], axis=0)
    for i in range(n_span):
        @pl.when(n // per_span == i)
        def _(i=i):
            _dsa_attn_block(wi_ref, k_ref, vt_ref, ki_ref, key_scr, bias_scr, qs_scr, qis_scr, og_scr,
                            top_k=top_k, seq=(i + 1) * span_keys)
    for g in range(og_scr.shape[0]):
        o_ref[:, g * hg * dh:(g + 1) * hg * dh] = og_scr[g]


def _dsa_attn_block(wi_ref, k_ref, vt_ref, ki_ref, key_full, bias_full, qs_scr, qis_scr, og_scr, *, top_k, seq):
    n = pl.program_id(1)
    qb = key_full.shape[1]
    key_scr = key_full.at[0:seq, :]
    bias_scr = bias_full.at[0:seq, :]
    kib = ki_ref[0:seq, :]
    wi = wi_ref[...]
    hgi = DSA_HEAD_GROUP
    idx = jnp.zeros((seq, qb), F32)
    for g in range(qis_scr.shape[0]):
        sc = lax.dot_general(kib, qis_scr[g], _NT, preferred_element_type=F32)
        for j in range(hgi):
            hh = g * hgi + j
            idx = idx + jnp.maximum(sc[:, j * qb:(j + 1) * qb], 0.0) * wi[hh:hh + 1, :]
    t_pos = n * qb + lax.broadcasted_iota(jnp.int32, (1, qb), 1)
    s_pos = lax.broadcasted_iota(jnp.int32, (seq, 1), 0)
    causal = s_pos <= t_pos
    idx = jnp.where(causal, idx, -jnp.inf)
    bits = lax.bitcast_convert_type(idx, jnp.int32)
    key_scr[...] = jnp.where(bits < 0, bits ^ jnp.int32(0x7FFFFFFF), bits)

    def count(pred):
        return _reduce_rows(jnp.where(pred, 1.0, 0.0), jnp.sum)

    int_min = jnp.int32(-2 ** 31)
    kf = float(top_k)
    thr0 = jnp.where(count(key_scr[...] >= 0) >= kf, jnp.int32(0), int_min)

    def thr_step(i, thr):
        cand = thr + lax.shift_left(jnp.int32(1), jnp.int32(30) - i)
        return jnp.where(count(key_scr[...] >= cand) >= kf, cand, thr)

    thr = lax.fori_loop(0, 31, thr_step, thr0)
    keys = key_scr[...]
    n_ge = count(keys >= thr)

    def tie_cut():
        need = kf - count(key_scr[...] > thr)
        n_bits = (seq - 1).bit_length()

        def cut_step(i, cut):
            cand = cut + lax.shift_left(jnp.int32(1), jnp.int32(n_bits - 1) - i)
            return jnp.where(count((key_scr[...] == thr) & (s_pos < cand)) < need, cand, cut)

        return lax.fori_loop(0, n_bits, cut_step, jnp.zeros((1, qb), jnp.int32))

    cut = lax.cond(jnp.max(n_ge) > kf, tie_cut, lambda: jnp.full((1, qb), seq, jnp.int32))
    sel = (keys > thr) | ((keys == thr) & (s_pos <= cut))
    bias_scr[...] = jnp.where(sel & causal, 0.0, -jnp.inf)

    hg = DSA_HEAD_GROUP

    def head_group(g, carry):
        lg = lax.dot_general(k_ref[0:seq, :], qs_scr[g], _NT, preferred_element_type=F32)
        lg = lg + jnp.concatenate([bias_scr[...]] * hg, axis=1)
        p = jnp.exp(lg - _reduce_rows(lg, jnp.max))
        den = _reduce_rows(p, jnp.sum)
        og = jnp.dot(vt_ref[:, 0:seq], p.astype(BF16), preferred_element_type=F32) / den
        pairs = []
        for j in range(0, hg, 2):
            pair = jnp.concatenate([og[:, j * qb:(j + 1) * qb], og[:, (j + 1) * qb:(j + 2) * qb]], axis=0)
            pairs.append(pair.T)
        og_scr[g] = jnp.concatenate(pairs, axis=1).astype(BF16)
        return carry

    lax.fori_loop(0, og_scr.shape[0], head_group, 0)


def _proj_res_kernel(a_ref, w_ref, x_ref, o_ref):
    o_ref[...] = x_ref[...] + jnp.dot(a_ref[...], w_ref[...], preferred_element_type=F32)


def _rope_tables(seq):
    half = ATTN_HEAD_DIM // 2
    inv = ROPE_THETA ** (-jnp.arange(half, dtype=F32) / half)
    ang = jnp.arange(seq, dtype=F32)[:, None] * inv
    cos, sin = jnp.cos(ang), jnp.sin(ang)
    zero, one = jnp.zeros_like(cos), jnp.ones_like(cos)
    cos64 = jnp.concatenate([cos, cos], axis=1)
    hi64 = jnp.concatenate([zero, sin], axis=1)
    lo64 = jnp.concatenate([-sin, zero], axis=1)
    full = jnp.stack([jnp.tile(t, (1, 2)) for t in (cos64, hi64, lo64)])
    ident = jnp.concatenate([one, one], axis=1)
    zero64 = jnp.concatenate([zero, zero], axis=1)
    wscale = ident * (IDX_HEADS ** -0.5 * IDX_DIM ** -0.5)
    halfs = jnp.stack([jnp.concatenate([cos64, ident], axis=1), jnp.concatenate([hi64, zero64], axis=1),
                       jnp.concatenate([lo64, zero64], axis=1), jnp.concatenate([cos64, wscale], axis=1)])
    return full, halfs


def _dsa_call(x, bsz, seq, g_mix, w_in, w_out):
    t, dm = x.shape
    dq = ATTN_HEADS * ATTN_HEAD_DIM
    dqi = IDX_HEADS * IDX_DIM
    o_k, o_v, o_qi = dq, dq + ATTN_HEAD_DIM, dq + 2 * ATTN_HEAD_DIM
    o_ki = o_qi + dqi
    o_wi = o_ki + IDX_DIM
    wq = w_in[:, :o_k].astype(BF16)
    wqi = w_in[:, o_qi:o_ki].astype(BF16)
    wsm = jnp.concatenate([w_in[:, o_k:o_qi], w_in[:, o_ki:],
                           jnp.zeros((dm, LANES - IDX_DIM - IDX_HEADS), w_in.dtype)], axis=1).astype(BF16)
    full, halfs = _rope_tables(seq)
    tm = 256
    nt = seq // tm
    row = lambda i: (i, 0)
    tab = lambda i: (0, i % nt, 0)
    q, qi, k, v, ki, wi = pl.pallas_call(
        _dsa_proj_kernel,
        grid=(t // tm,),
        in_specs=[
            pl.BlockSpec((tm, dm), row), _resident((1, dm)), _resident(wq.shape), _resident(wqi.shape),
            _resident(wsm.shape),
            pl.BlockSpec((3, tm, LANES), tab), pl.BlockSpec((4, tm, LANES), tab),
        ],
        out_specs=[
            pl.BlockSpec((tm, dq), row), pl.BlockSpec((tm, dqi), row),
            pl.BlockSpec((tm, ATTN_HEAD_DIM), row), pl.BlockSpec((tm, ATTN_HEAD_DIM), row),
            pl.BlockSpec((tm, IDX_DIM), row), pl.BlockSpec((tm, LANES), row),
        ],
        out_shape=[
            jax.ShapeDtypeStruct((t, dq), BF16), jax.ShapeDtypeStruct((t, dqi), BF16),
            jax.ShapeDtypeStruct((t, ATTN_HEAD_DIM), BF16), jax.ShapeDtypeStruct((t, ATTN_HEAD_DIM), BF16),
            jax.ShapeDtypeStruct((t, IDX_DIM), BF16), jax.ShapeDtypeStruct((t, LANES), F32),
        ],
        compiler_params=_params("parallel"),
        name="dsa_proj",
    )(x, g_mix.reshape(1, dm), wq, wqi, wsm, full, halfs)

    qb = QUERY_BLOCK
    nb = seq // qb
    top_k = min(TOPK_MAX, seq // 4)
    blk = lambda b, n: (b * nb + n, 0)
    per_b = lambda b, n: (b, 0, 0)
    wi_t = wi[:, IDX_DIM:IDX_DIM + IDX_HEADS].reshape(bsz, seq, IDX_HEADS).transpose(0, 2, 1)
    v_t = v.reshape(bsz, seq, ATTN_HEAD_DIM).transpose(0, 2, 1)
    o = pl.pallas_call(
        functools.partial(_dsa_attn_kernel, top_k=top_k, n_span=math.gcd(nb, 8)),
        grid=(bsz, nb),
        in_specs=[
            pl.BlockSpec((qb, dq), blk), pl.BlockSpec((qb, dqi), blk),
            pl.BlockSpec((None, IDX_HEADS, qb), lambda b, n: (b, 0, n)),
            pl.BlockSpec((None, seq, ATTN_HEAD_DIM), per_b), pl.BlockSpec((None, ATTN_HEAD_DIM, seq), per_b),
            pl.BlockSpec((None, seq, IDX_DIM), per_b),
        ],
        out_specs=pl.BlockSpec((qb, dq), blk),
        out_shape=jax.ShapeDtypeStruct((t, dq), BF16),
        scratch_shapes=[
            pltpu.VMEM((seq, qb), jnp.int32), pltpu.VMEM((seq, qb), F32),
            pltpu.VMEM((ATTN_HEADS // DSA_HEAD_GROUP, DSA_HEAD_GROUP * qb, ATTN_HEAD_DIM), BF16),
            pltpu.VMEM((IDX_HEADS // DSA_HEAD_GROUP, DSA_HEAD_GROUP * qb, IDX_DIM), BF16),
            pltpu.VMEM((ATTN_HEADS // DSA_HEAD_GROUP, qb, DSA_HEAD_GROUP * ATTN_HEAD_DIM), BF16),
        ],
        compiler_params=_params("parallel", "arbitrary"),
        name="dsa_attn",
    )(q, qi, wi_t, k.reshape(bsz, seq, -1), v_t, ki.reshape(bsz, seq, -1))

    tmo = 512
    return pl.pallas_call(
        _proj_res_kernel,
        grid=(t // tmo,),
        in_specs=[pl.BlockSpec((tmo, dq), row), _resident((dq, dm)), pl.BlockSpec((tmo, dm), row)],
        out_specs=pl.BlockSpec((tmo, dm), row),
        out_shape=jax.ShapeDtypeStruct((t, dm), F32),
        compiler_params=_params("parallel"),
        name="dsa_out",
    )(o, w_out.astype(BF16), x)


def kernel(x, ffn1_norm, ffn1_w_in, ffn1_w_out, mix_norm, ffn2_norm, ffn2_w_in, ffn2_w_out, ssd_in_proj, ssd_conv_w, ssd_conv_b, ssd_dt_bias, ssd_a_log, ssd_d, ssd_gate_norm, ssd_out_proj, s5_b_re, s5_b_im, s5_c_re, s5_c_im, s5_lam_re, s5_lam_im, s5_log_step, s5_d, s5_glu_w, s5_glu_b, dsa_in_proj, dsa_out_proj, final_norm):
    bsz, seq, dm = x.shape
    depth = ffn1_norm.shape[0]
    xt = x.reshape(bsz * seq, dm)
    for i in range(depth):
        j, kind = divmod(i, N_MIXERS)
        xt = _ffn_call(xt, ffn1_norm[i], ffn1_w_in[i].astype(BF16), ffn1_w_out[i].astype(BF16))
        if kind == 0:
            xt = _ssd_call(xt, bsz, seq, mix_norm[i], ssd_in_proj[j], ssd_conv_w[j], ssd_conv_b[j],
                           ssd_dt_bias[j], ssd_a_log[j], ssd_d[j], ssd_gate_norm[j], ssd_out_proj[j])
        elif kind == 1:
            xt = _s5_call(xt, bsz, seq, mix_norm[i], s5_b_re[j], s5_b_im[j], s5_c_re[j], s5_c_im[j],
                          s5_lam_re[j], s5_lam_im[j], s5_log_step[j], s5_d[j], s5_glu_w[j], s5_glu_b[j])
        else:
            xt = _dsa_call(xt, bsz, seq, mix_norm[i], dsa_in_proj[j], dsa_out_proj[j])
        post = final_norm if i == depth - 1 else None
        xt = _ffn_call(xt, ffn2_norm[i], ffn2_w_in[i].astype(BF16), ffn2_w_out[i].astype(BF16), post)
    return xt.reshape(bsz, seq, dm)
```

```python
import functools
import math

import jax
import jax.numpy as jnp
from jax import lax
from jax.experimental import pallas as pl
from jax.experimental.pallas import tpu as pltpu

F32 = jnp.float32
BF16 = jnp.bfloat16

NORM_EPS = 1e-6
N_MIXERS = 3

SSD_HEAD_DIM = 64
SSD_GROUPS = 8
SSD_HEADS_PER_GROUP = 4
SSD_HEADS = SSD_GROUPS * SSD_HEADS_PER_GROUP
SSD_STATE = 128
SSD_INNER = SSD_HEADS * SSD_HEAD_DIM
SSD_CONV = 4
SSD_CHUNK_MAX = 256
SSD_GROUP_WIDTH = SSD_HEADS_PER_GROUP * SSD_HEAD_DIM

S5_GROUP_WIDTH = 16
S5_STATE = 64
S5_SUB = 16

ATTN_HEADS = 16
ATTN_HEAD_DIM = 64
IDX_HEADS = 8
IDX_DIM = 64
TOPK_MAX = 256
QUERY_BLOCK = 128
ROPE_THETA = 10000.0
DSA_HEAD_GROUP = 4

LANES = 128
SUBLANES = 8
VMEM_LIMIT_BYTES = 56 * 1024 * 1024

_NT = (((1,), (1,)), ((), ()))


def _params(*semantics):
    return pltpu.CompilerParams(dimension_semantics=semantics, vmem_limit_bytes=VMEM_LIMIT_BYTES)


def _rms(x, g):
    ms = jnp.mean(x * x, axis=-1, keepdims=True)
    return x * lax.rsqrt(ms + NORM_EPS) * g


def _silu(x):
    return x * jax.nn.sigmoid(x)


def _softplus(x):
    return jnp.maximum(x, 0.0) + jnp.log1p(jnp.exp(-jnp.abs(x)))


def _resident(shape):
    zeros = (0,) * len(shape)
    return pl.BlockSpec(shape, lambda *_: zeros, pipeline_mode=pl.Buffered(1))


FFN_CHUNK = 256


def _ffn_kernel(x_ref, g_ref, win_ref, wout_ref, pg_ref, o_ref, *, post_norm):
    x = x_ref[...]
    h = _rms(x, g_ref[...]).astype(BF16)
    f = wout_ref.shape[0]
    acc = jnp.zeros(x.shape, F32)
    for j in range(f // FFN_CHUNK):
        lo = j * FFN_CHUNK
        gate = jnp.dot(h, win_ref[:, lo:lo + FFN_CHUNK], preferred_element_type=F32)
        up = jnp.dot(h, win_ref[:, f + lo:f + lo + FFN_CHUNK], preferred_element_type=F32)
        act = (_silu(gate) * up).astype(BF16)
        acc = acc + jnp.dot(act, wout_ref[lo:lo + FFN_CHUNK, :], preferred_element_type=F32)
    y = x + 0.5 * acc
    if post_norm:
        y = _rms(y, pg_ref[...])
    o_ref[...] = y


def _ffn_call(x, g, w_in, w_out, post_g=None):
    t, d = x.shape
    f = w_out.shape[0]
    tm = 1024
    assert t % tm == 0 and f % FFN_CHUNK == 0
    post_norm = post_g is not None
    pg = post_g if post_norm else g
    return pl.pallas_call(
        functools.partial(_ffn_kernel, post_norm=post_norm),
        grid=(t // tm,),
        in_specs=[
            pl.BlockSpec((tm, d), lambda i: (i, 0)),
            _resident((1, d)), _resident(w_in.shape), _resident(w_out.shape), _resident((1, d)),
        ],
        out_specs=pl.BlockSpec((tm, d), lambda i: (i, 0)),
        out_shape=jax.ShapeDtypeStruct((t, d), F32),
        compiler_params=_params("parallel"),
        name="ffn",
    )(x, g.reshape(1, d), w_in, w_out, pg.reshape(1, d))


def _split3(a):
    hi = a.astype(BF16)
    r1 = a - hi.astype(F32)
    mid = r1.astype(BF16)
    lo = (r1 - mid.astype(F32)).astype(BF16)
    return hi, mid, lo


def _ssd_kernel(x_ref, gmix_ref, wz_ref, wxbc_ref, wdt_ref, wdtt_ref, convw_ref, convb_ref,
                dtbc_ref, dtbr_ref, ac_ref, ar_ref, dvec_ref, gn_ref, wout_ref, selg_ref, selh_ref,
                o_ref,
                raw_scr, carry_scr, xs_scr, bm_scr, cm_scr, z_scr, yn_scr, dtc_scr, dtr_scr, state_scr,
                *, q, n_ch):
    pad = SUBLANES
    rows = n_ch * q
    gw = SSD_GROUP_WIDTH
    hd_dim = SSD_HEAD_DIM
    n_state = SSD_STATE
    hpg = SSD_HEADS_PER_GROUP

    @pl.when(pl.program_id(1) == 0)
    def _():
        state_scr[...] = jnp.zeros_like(state_scr)
        carry_scr[...] = jnp.zeros_like(carry_scr)

    x = x_ref[...]
    h = _rms(x, gmix_ref[...]).astype(BF16)
    z_scr[...] = jnp.dot(h, wz_ref[...], preferred_element_type=F32)

    cw = raw_scr.shape[1]
    n_x = SSD_INNER // cw
    n_b = SSD_GROUPS * n_state // cw
    for j in range(wxbc_ref.shape[1] // cw):
        sl = slice(j * cw, (j + 1) * cw)
        raw = raw_scr
        raw[0:pad, :] = carry_scr[:, sl]
        raw[pad:pad + rows, :] = jnp.dot(h, wxbc_ref[:, sl], preferred_element_type=F32)
        carry_scr[:, sl] = raw[rows:rows + pad, :]
        w = convw_ref[:, sl]
        acc = raw[pad - 3:pad - 3 + rows, :] * w[0:1]
        acc = acc + raw[pad - 2:pad - 2 + rows, :] * w[1:2]
        acc = acc + raw[pad - 1:pad - 1 + rows, :] * w[2:3]
        acc = acc + raw[pad:pad + rows, :] * w[3:4]
        act = _silu(acc + convb_ref[:, sl])
        if j < n_x:
            xs_scr[:, sl] = act
        elif j < n_x + n_b:
            bm_scr[:, (j - n_x) * cw:(j - n_x + 1) * cw] = act
        else:
            cm_scr[:, (j - n_x - n_b) * cw:(j - n_x - n_b + 1) * cw] = act

    dtc_scr[...] = _softplus(jnp.dot(h, wdt_ref[...], preferred_element_type=F32) + dtbc_ref[...])
    dt_rows = _softplus(lax.dot_general(wdtt_ref[...], h, _NT, preferred_element_type=F32) + dtbr_ref[...])
    for ci in range(n_ch):
        dtr_scr[ci] = dt_rows[:, ci * q:(ci + 1) * q]
    row = lax.broadcasted_iota(jnp.int32, (q, q), 0)
    col = lax.broadcasted_iota(jnp.int32, (q, q), 1)
    causal = col <= row
    tri_low = jnp.where(causal, 1.0, 0.0).astype(BF16)
    tri_up = jnp.where(row <= col, 1.0, 0.0).astype(BF16)

    def expand(a, sel):
        return sum(jnp.dot(p, sel, preferred_element_type=F32) for p in _split3(a))

    def chunk(ci, carry):
        r = pl.ds(pl.multiple_of(ci * q, q), q)
        dt_r = dtr_scr[ci]
        a_c = dtc_scr[r, :] * ac_ref[...]
        a_r = dt_r * ar_ref[...]
        cs_c = sum(jnp.dot(tri_low, p, preferred_element_type=F32) for p in _split3(a_c))
        cs_r = sum(jnp.dot(p, tri_up, preferred_element_type=F32) for p in _split3(a_r))
        e_c = jnp.exp(cs_c)
        cs_end = cs_r[:, q - 1:q]
        f_r = dt_r * jnp.exp(cs_end - cs_r)
        e_end = jnp.exp(cs_end)
        for g in range(SSD_GROUPS):
            bg = bm_scr[r, g * n_state:(g + 1) * n_state].astype(BF16)
            cg = cm_scr[r, g * n_state:(g + 1) * n_state].astype(BF16)
            cb = lax.dot_general(cg, bg, _NT, preferred_element_type=F32)
            xs_g = xs_scr[r, g * gw:(g + 1) * gw]
            xs_gb = xs_g.astype(BF16)
            xs_gt = xs_g.T
            st_g = state_scr[g]
            y_off = lax.dot_general(cg, st_g.astype(BF16), _NT, preferred_element_type=F32)
            cs_cols = expand(cs_c, selh_ref[g])
            ys, ws, decs = [], [], []
            for k in range(hpg):
                hd = g * hpg + k
                cs_col = cs_cols[:, k * LANES:(k + 1) * LANES]
                seg = jnp.concatenate([cs_col] * (q // LANES), axis=1) - cs_r[hd:hd + 1, :]
                decay = jnp.exp(jnp.where(causal, seg, -jnp.inf))
                m = (cb * decay * dt_r[hd:hd + 1, :]).astype(BF16)
                ys.append(jnp.dot(m, xs_gb[:, k * hd_dim:(k + 1) * hd_dim], preferred_element_type=F32))
                ws.append(xs_gt[k * hd_dim:(k + 1) * hd_dim, :] * f_r[hd:hd + 1, :])
                decs.append(jnp.broadcast_to(e_end[hd:hd + 1, :], (hd_dim, n_state)))
            wt = jnp.concatenate(ws, axis=0)
            upd = jnp.dot(wt.astype(BF16), bg, preferred_element_type=F32)
            state_scr[g] = st_g * jnp.concatenate(decs, axis=0) + upd
            y_g = jnp.concatenate(ys, axis=1) + y_off * expand(e_c, selg_ref[g])
            y_g = y_g + dvec_ref[:, g * gw:(g + 1) * gw] * xs_g
            yg = y_g * _silu(z_scr[r, g * gw:(g + 1) * gw])
            yg = yg * lax.rsqrt(jnp.mean(yg * yg, axis=-1, keepdims=True) + NORM_EPS)
            yn_scr[r, g * gw:(g + 1) * gw] = (yg * gn_ref[:, g * gw:(g + 1) * gw]).astype(BF16)
        return carry

    lax.fori_loop(0, n_ch, chunk, 0)
    o_ref[...] = x + jnp.dot(yn_scr[...], wout_ref[...], preferred_element_type=F32)


def _ssd_call(x, bsz, seq, g_mix, in_proj, conv_w, conv_b, dt_bias, a_log, d, gate_norm, out_proj):
    t, dm = x.shape
    q = math.gcd(seq, SSD_CHUNK_MAX)
    nc = seq // q
    conv_dim = SSD_INNER + 2 * SSD_GROUPS * SSD_STATE
    wz = in_proj[:, :SSD_INNER].astype(BF16)
    wxbc = in_proj[:, SSD_INNER:SSD_INNER + conv_dim].astype(BF16)
    wdt = in_proj[:, SSD_INNER + conv_dim:]
    wdt_c = jnp.pad(wdt, ((0, 0), (0, LANES - SSD_HEADS))).astype(BF16)
    wdt_r = wdt.T.astype(BF16)
    a = -jnp.exp(a_log.astype(F32))
    lane_pad = (0, LANES - SSD_HEADS)
    head_of = jnp.arange(LANES)
    grp_heads = jnp.arange(SSD_GROUPS) * SSD_HEADS_PER_GROUP
    args = (
        x, g_mix.reshape(1, dm), wz, wxbc, wdt_c, wdt_r, conv_w, conv_b.reshape(1, conv_dim),
        jnp.pad(dt_bias, lane_pad).reshape(1, LANES), dt_bias.reshape(SSD_HEADS, 1),
        jnp.pad(a, lane_pad).reshape(1, LANES), a.reshape(SSD_HEADS, 1),
        jnp.repeat(d, SSD_HEAD_DIM).reshape(1, SSD_INNER), gate_norm.reshape(1, SSD_INNER),
        out_proj.astype(BF16),
        (head_of[:, None] == (grp_heads[:, None, None] + jnp.arange(SSD_GROUP_WIDTH) // SSD_HEAD_DIM)).astype(BF16),
        (head_of[:, None] == (grp_heads[:, None, None] + jnp.arange(SSD_HEADS_PER_GROUP * LANES) // LANES)).astype(BF16),
    )
    n_ch = math.gcd(nc, 2)
    steps = nc // n_ch
    rows = n_ch * q
    x_spec = pl.BlockSpec((rows, dm), lambda b, c: (b * steps + c, 0))
    in_specs = [x_spec] + [_resident(a_.shape) for a_ in args[1:]]
    return pl.pallas_call(
        functools.partial(_ssd_kernel, q=q, n_ch=n_ch),
        grid=(bsz, steps),
        in_specs=in_specs,
        out_specs=x_spec,
        out_shape=jax.ShapeDtypeStruct((t, dm), F32),
        scratch_shapes=[
            pltpu.VMEM((rows + 2 * SUBLANES, 4 * LANES), F32),
            pltpu.VMEM((SUBLANES, conv_dim), F32),
            pltpu.VMEM((rows, SSD_INNER), F32),
            pltpu.VMEM((rows, SSD_GROUPS * SSD_STATE), F32),
            pltpu.VMEM((rows, SSD_GROUPS * SSD_STATE), F32),
            pltpu.VMEM((rows, SSD_INNER), F32),
            pltpu.VMEM((rows, SSD_INNER), BF16),
            pltpu.VMEM((rows, LANES), F32),
            pltpu.VMEM((n_ch, SSD_HEADS, q), F32),
            pltpu.VMEM((SSD_GROUPS, SSD_GROUP_WIDTH, SSD_STATE), F32),
        ],
        compiler_params=_params("arbitrary", "arbitrary"),
        name="ssd_mixer",
    )(*args)


S5_GB = LANES // S5_GROUP_WIDTH


def _s5_pack_kernel(x_ref, g_ref, o_ref, h_scr):
    h = _rms(x_ref[...], g_ref[...])
    rows = o_ref.shape[1]
    for gb in range(o_ref.shape[0]):
        h_scr[gb] = h[:, gb * LANES:(gb + 1) * LANES]
    for s in range(S5_SUB):
        for gb in range(o_ref.shape[0]):
            piece = h_scr[gb, pl.ds(s, rows, stride=S5_SUB), :]
            o_ref[gb, :, s * LANES:(s + 1) * LANES] = piece.astype(o_ref.dtype)


def _s5_scan_kernel(u_ref, mop_ref, gop_ref, cop_ref, lam_ref, y_ref, g_scr, hp_scr, *, n_bt, n_sub):
    n_half = g_scr.shape[0] // 2
    tile = 2 * LANES
    u = u_ref[...]
    g_in = jnp.dot(u, gop_ref[...], preferred_element_type=F32)
    for c in range(2 * n_half):
        g_scr[c] = g_in[:, c * LANES:(c + 1) * LANES]
    lam = lam_ref[...]
    lr = [jnp.broadcast_to(lam[0:1, c * LANES:(c + 1) * LANES], (n_bt, LANES)) for c in range(n_half)]
    li = [jnp.broadcast_to(lam[1:2, c * LANES:(c + 1) * LANES], (n_bt, LANES)) for c in range(n_half)]
    hr = [jnp.zeros((n_bt, LANES), F32)] * n_half
    hi = [jnp.zeros((n_bt, LANES), F32)] * n_half
    for j in range(n_sub):
        rows = pl.ds(j, n_bt, stride=n_sub)
        for c in range(n_half):
            hp_scr[c, rows, :] = hr[c]
            hp_scr[n_half + c, rows, :] = hi[c]
            gr, gi = g_scr[c, rows, :], g_scr[n_half + c, rows, :]
            hr[c], hi[c] = lr[c] * hr[c] - li[c] * hi[c] + gr, lr[c] * hi[c] + li[c] * hr[c] + gi
    h_in = jnp.concatenate([hp_scr[c] for c in range(2 * n_half)], axis=1).astype(BF16)
    y_inter = jnp.dot(h_in, cop_ref[...], preferred_element_type=F32)
    n_t = u.shape[1] // tile
    for tp in range(n_t):
        k_hi = (tp + 1) * tile
        y_intra = jnp.dot(u[:, :k_hi], mop_ref[(n_t - 1 - tp) * tile:, :], preferred_element_type=F32)
        y_ref[:, tp * tile:(tp + 1) * tile] = y_inter[:, tp * tile:(tp + 1) * tile] + y_intra


def _s5_operators(b_re, b_im, c_re, c_im, lam_re, lam_im, log_step):
    hp = lax.Precision.HIGHEST
    s, gb, w = S5_SUB, S5_GB, S5_GROUP_WIDTH
    n_g, n_p = lam_re.shape
    n_blk = n_g // gb
    lam = lax.complex(lam_re.astype(F32), lam_im.astype(F32))
    step = jnp.exp(log_step.astype(F32))[:, None]
    tau = jnp.arange(s + 1, dtype=F32)[:, None, None]
    lam_pow = jnp.exp((lam * step)[None] * tau)
    b_bar = ((lam_pow[1] - 1.0) / lam)[..., None] * lax.complex(b_re.astype(F32), b_im.astype(F32))
    cc = lax.complex(c_re.astype(F32), c_im.astype(F32))
    eye = jnp.eye(gb, dtype=F32)
    kern = jnp.einsum('gcp,tgp,gpd->tgcd', cc, lam_pow[:s], b_bar, precision=hp).real
    bd = jnp.einsum('tagcd,gh->atgdhc', kern.reshape(s, n_blk, gb, w, w), eye).reshape(n_blk, s, gb * w, gb * w)
    bd = jnp.concatenate([bd, jnp.zeros_like(bd[:, :1])], axis=1)
    two = jnp.arange(2)
    tile_lag = jnp.arange(s // 2 - 1, -1, -1)
    lag = 2 * tile_lag[:, None, None] + two[None, None, :] - two[None, :, None]
    mop = bd[:, jnp.where(lag < 0, s, lag)]
    mop = mop.transpose(0, 1, 2, 4, 3, 5).reshape(n_blk, s * gb * w, 2 * gb * w)
    ti = jnp.arange(s)
    gm = (lam_pow[s - 1 - ti][:, :, :, None] * b_bar[None]).reshape(s, n_blk, gb, n_p, w)
    gop = jnp.stack([jnp.einsum('sagpd,gh->asgdhp', part, eye) for part in (gm.real, gm.imag)], axis=4)
    gop = gop.reshape(n_blk, s * gb * w, 2 * gb * n_p)
    em = (cc[None] * lam_pow[1:s + 1][:, :, None, :]).reshape(s, n_blk, gb, w, n_p)
    cop = jnp.stack([jnp.einsum('tagcp,gh->agpthc', part, eye) for part in (em.real, -em.imag)], axis=1)
    cop = cop.reshape(n_blk, 2 * gb * n_p, s * gb * w)
    lam_s = lam_pow[s].reshape(n_blk, gb * n_p)
    lam_blk = jnp.stack([lam_s.real, lam_s.imag], axis=1)
    return mop.astype(BF16), gop.astype(BF16), cop.astype(BF16), lam_blk


def _glu_kernel(x_ref, y_ref, g_ref, d_ref, w_ref, b_ref, o_ref, y_scr):
    x = x_ref[...]
    d_model = x.shape[1]
    rows = y_ref.shape[1]
    for s in range(S5_SUB):
        for gb in range(y_ref.shape[0]):
            y_scr[gb, pl.ds(s, rows, stride=S5_SUB), :] = y_ref[gb, :, s * LANES:(s + 1) * LANES]
    y = jnp.concatenate([y_scr[gb] for gb in range(y_ref.shape[0])], axis=1)
    h = _rms(x, g_ref[...])
    act = jax.nn.gelu(y + d_ref[...] * h).astype(BF16)
    vg = jnp.dot(act, w_ref[...], preferred_element_type=F32) + b_ref[...]
    o_ref[...] = x + vg[:, :d_model] * jax.nn.sigmoid(vg[:, d_model:])


def _s5_call(x, bsz, seq, g_mix, b_re, b_im, c_re, c_im, lam_re, lam_im, log_step, d, glu_w, glu_b):
    t, dm = x.shape
    s = S5_SUB
    n_blk = dm // LANES
    n_sub = seq // s
    rows = t // s
    width = s * LANES
    state_w = 2 * S5_GB * S5_STATE
    n_bt = 4
    tm = 512
    assert seq % s == 0 and bsz % n_bt == 0 and t % tm == 0
    mop, gop, cop, lam_blk = _s5_operators(b_re, b_im, c_re, c_im, lam_re, lam_im, log_step)
    tok = lambda i: (i, 0)
    packed = lambda i: (0, i, 0)
    u = pl.pallas_call(
        _s5_pack_kernel,
        grid=(t // tm,),
        in_specs=[pl.BlockSpec((tm, dm), tok), _resident((1, dm))],
        out_specs=pl.BlockSpec((n_blk, tm // s, width), packed),
        out_shape=jax.ShapeDtypeStruct((n_blk, rows, width), BF16),
        scratch_shapes=[pltpu.VMEM((n_blk, tm, LANES), F32)],
        compiler_params=_params("parallel"),
        name="s5_pack",
    )(x, g_mix.reshape(1, dm))
    rt = n_bt * n_sub
    per_blk = lambda a, r: (a, 0, 0)
    once = pl.Buffered(1)
    y = pl.pallas_call(
        functools.partial(_s5_scan_kernel, n_bt=n_bt, n_sub=n_sub),
        grid=(n_blk, rows // rt),
        in_specs=[
            pl.BlockSpec((None, rt, width), lambda a, r: (a, r, 0)),
            pl.BlockSpec((None, width, 2 * LANES), per_blk, pipeline_mode=once),
            pl.BlockSpec((None, width, state_w), per_blk, pipeline_mode=once),
            pl.BlockSpec((None, state_w, width), per_blk, pipeline_mode=once),
            pl.BlockSpec((None, 2, state_w // 2), per_blk),
        ],
        out_specs=pl.BlockSpec((None, rt, width), lambda a, r: (a, r, 0)),
        out_shape=jax.ShapeDtypeStruct((n_blk, rows, width), F32),
        scratch_shapes=[pltpu.VMEM((state_w // LANES, rt, LANES), F32)] * 2,
        compiler_params=_params("arbitrary", "arbitrary"),
        name="s5_scan",
    )(u, mop, gop, cop, lam_blk)
    return pl.pallas_call(
        _glu_kernel,
        grid=(t // tm,),
        in_specs=[
            pl.BlockSpec((tm, dm), tok),
            pl.BlockSpec((n_blk, tm // s, width), packed),
            _resident((1, dm)), _resident((1, dm)), _resident((dm, 2 * dm)), _resident((1, 2 * dm)),
        ],
        out_specs=pl.BlockSpec((tm, dm), tok),
        out_shape=jax.ShapeDtypeStruct((t, dm), F32),
        scratch_shapes=[pltpu.VMEM((n_blk, tm, LANES), F32)],
        compiler_params=_params("parallel"),
        name="s5_glu",
    )(x, y, g_mix.reshape(1, dm), d.reshape(1, dm), glu_w.astype(BF16), glu_b.reshape(1, 2 * dm))


def _rope128(tile, cos, sin_hi, sin_lo):
    return tile * cos + pltpu.roll(tile, 32, 1) * sin_hi + pltpu.roll(tile, LANES - 32, 1) * sin_lo


def _dsa_proj_kernel(x_ref, g_ref, wq_ref, wqi_ref, wsm_ref, tab_ref, tabh_ref,
                     q_ref, qi_ref, k_ref, v_ref, ki_ref, wi_ref):
    h = _rms(x_ref[...], g_ref[...]).astype(BF16)
    cos, s_hi, s_lo = tab_ref[0], tab_ref[1], tab_ref[2]
    q = jnp.dot(h, wq_ref[...], preferred_element_type=F32)
    scale = ATTN_HEAD_DIM ** -0.5
    for j in range(q.shape[1] // LANES):
        sl = slice(j * LANES, (j + 1) * LANES)
        q_ref[:, sl] = (_rope128(q[:, sl], cos, s_hi, s_lo) * scale).astype(BF16)
    qi = jnp.dot(h, wqi_ref[...], preferred_element_type=F32)
    for j in range(qi.shape[1] // LANES):
        sl = slice(j * LANES, (j + 1) * LANES)
        qi_ref[:, sl] = _rope128(qi[:, sl], cos, s_hi, s_lo).astype(BF16)
    sm = jnp.dot(h, wsm_ref[...], preferred_element_type=F32)
    kv = _rope128(sm[:, :LANES], tabh_ref[0], tabh_ref[1], tabh_ref[2])
    kw = _rope128(sm[:, LANES:], tabh_ref[3], tabh_ref[1], tabh_ref[2])
    k_ref[...] = kv[:, :ATTN_HEAD_DIM].astype(BF16)
    v_ref[...] = kv[:, ATTN_HEAD_DIM:].astype(BF16)
    ki_ref[...] = kw[:, :IDX_DIM].astype(BF16)
    wi_ref[...] = kw


def _reduce_rows(x, op):
    slab = 8 * SUBLANES
    part = op(x.reshape(x.shape[0] // slab, slab, x.shape[1]), axis=0)
    return op(part, axis=0, keepdims=True)


def _dsa_attn_kernel(q_ref, qi_ref, wi_ref, k_ref, vt_ref, ki_ref, o_ref,
                     key_scr, bias_scr, qs_scr, qis_scr, og_scr, lga_scr, lgb_scr, *, top_k, span_ends):
    n = pl.program_id(1)
    qb = key_scr.shape[1]
    hg, dh = DSA_HEAD_GROUP, ATTN_HEAD_DIM
    for g in range(qs_scr.shape[0]):
        qs_scr[g] = jnp.concatenate([q_ref[:, (g * hg + j) * dh:(g * hg + j + 1) * dh] for j in range(hg)], axis=0)
    for g in range(qis_scr.shape[0]):
        qis_scr[g] = jnp.concatenate(
            [qi_ref[:, (g * hg + j) * IDX_DIM:(g * hg + j + 1) * IDX_DIM] for j in range(hg)], axis=0)
    for lo, hi in zip((0,) + span_ends[:-1], span_ends):
        @pl.when((n >= lo) & (n < hi))
        def _(hi=hi):
            _dsa_attn_block(wi_ref, k_ref, vt_ref, ki_ref, key_scr, bias_scr, qs_scr, qis_scr, og_scr,
                            lga_scr, lgb_scr,
                            top_k=top_k, seq=hi * qb)
    for g in range(og_scr.shape[0]):
        o_ref[:, g * hg * dh:(g + 1) * hg * dh] = og_scr[g]


def _dsa_attn_block(wi_ref, k_ref, vt_ref, ki_ref, key_full, bias_full, qs_scr, qis_scr, og_scr,
                    lga_scr, lgb_scr, *, top_k, seq):
    n = pl.program_id(1)
    qb = key_full.shape[1]
    key_scr = key_full.at[0:seq, :]
    bias_scr = bias_full.at[0:seq, :]
    kib = ki_ref[0:seq, :]
    wi = wi_ref[...]
    hgi = DSA_HEAD_GROUP
    idx = jnp.zeros((seq, qb), F32)
    for g in range(qis_scr.shape[0]):
        sc = lax.dot_general(kib, qis_scr[g], _NT, preferred_element_type=F32)
        for j in range(hgi):
            hh = g * hgi + j
            idx = idx + jnp.maximum(sc[:, j * qb:(j + 1) * qb], 0.0) * wi[hh:hh + 1, :]
    t_pos = n * qb + lax.broadcasted_iota(jnp.int32, (1, qb), 1)
    s_pos = lax.broadcasted_iota(jnp.int32, (seq, 1), 0)
    causal = s_pos <= t_pos
    idx = jnp.where(causal, idx, -jnp.inf)
    bits = lax.bitcast_convert_type(idx, jnp.int32)
    key_scr[...] = jnp.where(bits < 0, bits ^ jnp.int32(0x7FFFFFFF), bits)

    def count(pred):
        return _reduce_rows(jnp.where(pred, 1.0, 0.0), jnp.sum)

    int_min = jnp.int32(-2 ** 31)
    kf = float(top_k)
    thr0 = jnp.where(count(key_scr[...] >= 0) >= kf, jnp.int32(0), int_min)

    def thr_step(i, thr):
        cand = thr + lax.shift_left(jnp.int32(1), jnp.int32(30) - i)
        return jnp.where(count(key_scr[...] >= cand) >= kf, cand, thr)

    thr = lax.fori_loop(0, 31, thr_step, thr0)
    keys = key_scr[...]
    n_ge = count(keys >= thr)

    def tie_cut():
        need = kf - count(key_scr[...] > thr)
        n_bits = (seq - 1).bit_length()

        def cut_step(i, cut):
            cand = cut + lax.shift_left(jnp.int32(1), jnp.int32(n_bits - 1) - i)
            return jnp.where(count((key_scr[...] == thr) & (s_pos < cand)) < need, cand, cut)

        return lax.fori_loop(0, n_bits, cut_step, jnp.zeros((1, qb), jnp.int32))

    cut = lax.cond(jnp.max(n_ge) > kf, tie_cut, lambda: jnp.full((1, qb), seq, jnp.int32))
    sel = (keys > thr) | ((keys == thr) & (s_pos <= cut))
    bias_scr[...] = jnp.where(sel & causal, 0.0, -jnp.inf)

    hg = DSA_HEAD_GROUP

    n_grp = og_scr.shape[0]
    lg_even = lga_scr.at[0:seq, :]
    lg_odd = lgb_scr.at[0:seq, :]

    def scores(g):
        return lax.dot_general(k_ref[0:seq, :], qs_scr[g], _NT, preferred_element_type=F32)

    def finish(g, lg):
        lg = lg + jnp.concatenate([bias_scr[...]] * hg, axis=1)
        p = jnp.exp((lg - _reduce_rows(lg, jnp.max)).astype(BF16))
        ov = jnp.dot(vt_ref[:, 0:seq], p, preferred_element_type=F32)
        og = ov[0:ATTN_HEAD_DIM, :] / ov[ATTN_HEAD_DIM:ATTN_HEAD_DIM + 1, :]
        pairs = []
        for j in range(0, hg, 2):
            pair = jnp.concatenate([og[:, j * qb:(j + 1) * qb], og[:, (j + 1) * qb:(j + 2) * qb]], axis=0)
            pairs.append(pair.T)
        og_scr[g] = jnp.concatenate(pairs, axis=1).astype(BF16)

    lg_even[...] = scores(0)

    def group_pair(i, carry):
        g = 2 * i
        lg_odd[...] = scores(g + 1)
        finish(g, lg_even[...])
        lg_even[...] = scores(jnp.minimum(g + 2, n_grp - 1))
        finish(g + 1, lg_odd[...])
        return carry

    lax.fori_loop(0, n_grp // 2, group_pair, 0)


def _proj_res_kernel(a_ref, w_ref, x_ref, o_ref):
    o_ref[...] = x_ref[...] + jnp.dot(a_ref[...], w_ref[...], preferred_element_type=F32)


def _rope_tables(seq):
    half = ATTN_HEAD_DIM // 2
    inv = ROPE_THETA ** (-jnp.arange(half, dtype=F32) / half)
    ang = jnp.arange(seq, dtype=F32)[:, None] * inv
    cos, sin = jnp.cos(ang), jnp.sin(ang)
    zero, one = jnp.zeros_like(cos), jnp.ones_like(cos)
    cos64 = jnp.concatenate([cos, cos], axis=1)
    hi64 = jnp.concatenate([zero, sin], axis=1)
    lo64 = jnp.concatenate([-sin, zero], axis=1)
    full = jnp.stack([jnp.tile(t, (1, 2)) for t in (cos64, hi64, lo64)])
    ident = jnp.concatenate([one, one], axis=1)
    zero64 = jnp.concatenate([zero, zero], axis=1)
    wscale = ident * (IDX_HEADS ** -0.5 * IDX_DIM ** -0.5)
    halfs = jnp.stack([jnp.concatenate([cos64, ident], axis=1), jnp.concatenate([hi64, zero64], axis=1),
                       jnp.concatenate([lo64, zero64], axis=1), jnp.concatenate([cos64, wscale], axis=1)])
    return full, halfs


def _dsa_call(x, bsz, seq, g_mix, w_in, w_out):
    t, dm = x.shape
    dq = ATTN_HEADS * ATTN_HEAD_DIM
    dqi = IDX_HEADS * IDX_DIM
    o_k, o_v, o_qi = dq, dq + ATTN_HEAD_DIM, dq + 2 * ATTN_HEAD_DIM
    o_ki = o_qi + dqi
    o_wi = o_ki + IDX_DIM
    wq = w_in[:, :o_k].astype(BF16)
    wqi = w_in[:, o_qi:o_ki].astype(BF16)
    wsm = jnp.concatenate([w_in[:, o_k:o_qi], w_in[:, o_ki:],
                           jnp.zeros((dm, LANES - IDX_DIM - IDX_HEADS), w_in.dtype)], axis=1).astype(BF16)
    full, halfs = _rope_tables(seq)
    tm = 256
    nt = seq // tm
    row = lambda i: (i, 0)
    tab = lambda i: (0, i % nt, 0)
    q, qi, k, v, ki, wi = pl.pallas_call(
        _dsa_proj_kernel,
        grid=(t // tm,),
        in_specs=[
            pl.BlockSpec((tm, dm), row), _resident((1, dm)), _resident(wq.shape), _resident(wqi.shape),
            _resident(wsm.shape),
            pl.BlockSpec((3, tm, LANES), tab), pl.BlockSpec((4, tm, LANES), tab),
        ],
        out_specs=[
            pl.BlockSpec((tm, dq), row), pl.BlockSpec((tm, dqi), row),
            pl.BlockSpec((tm, ATTN_HEAD_DIM), row), pl.BlockSpec((tm, ATTN_HEAD_DIM), row),
            pl.BlockSpec((tm, IDX_DIM), row), pl.BlockSpec((tm, LANES), row),
        ],
        out_shape=[
            jax.ShapeDtypeStruct((t, dq), BF16), jax.ShapeDtypeStruct((t, dqi), BF16),
            jax.ShapeDtypeStruct((t, ATTN_HEAD_DIM), BF16), jax.ShapeDtypeStruct((t, ATTN_HEAD_DIM), BF16),
            jax.ShapeDtypeStruct((t, IDX_DIM), BF16), jax.ShapeDtypeStruct((t, LANES), F32),
        ],
        compiler_params=_params("parallel"),
        name="dsa_proj",
    )(x, g_mix.reshape(1, dm), wq, wqi, wsm, full, halfs)

    qb = QUERY_BLOCK
    nb = seq // qb
    top_k = min(TOPK_MAX, seq // 4)
    blk = lambda b, n: (b * nb + n, 0)
    per_b = lambda b, n: (b, 0, 0)
    span_ends = tuple(sorted({e for e in (nb // 8, nb // 4, 3 * nb // 8, nb // 2, 3 * nb // 4, nb) if e > 0}))
    wi_t = wi[:, IDX_DIM:IDX_DIM + IDX_HEADS].reshape(bsz, seq, IDX_HEADS).transpose(0, 2, 1)
    v_t = v.reshape(bsz, seq, ATTN_HEAD_DIM).transpose(0, 2, 1)
    ones_rows = jnp.zeros((bsz, 2 * SUBLANES, seq), BF16).at[:, 0, :].set(1.0)
    v_t = jnp.concatenate([v_t, ones_rows], axis=1)
    o = pl.pallas_call(
        functools.partial(_dsa_attn_kernel, top_k=top_k, span_ends=span_ends),
        grid=(bsz, nb),
        in_specs=[
            pl.BlockSpec((qb, dq), blk), pl.BlockSpec((qb, dqi), blk),
            pl.BlockSpec((None, IDX_HEADS, qb), lambda b, n: (b, 0, n)),
            pl.BlockSpec((None, seq, ATTN_HEAD_DIM), per_b),
            pl.BlockSpec((None, ATTN_HEAD_DIM + 2 * SUBLANES, seq), per_b),
            pl.BlockSpec((None, seq, IDX_DIM), per_b),
        ],
        out_specs=pl.BlockSpec((qb, dq), blk),
        out_shape=jax.ShapeDtypeStruct((t, dq), BF16),
        scratch_shapes=[
            pltpu.VMEM((seq, qb), jnp.int32), pltpu.VMEM((seq, qb), F32),
            pltpu.VMEM((ATTN_HEADS // DSA_HEAD_GROUP, DSA_HEAD_GROUP * qb, ATTN_HEAD_DIM), BF16),
            pltpu.VMEM((IDX_HEADS // DSA_HEAD_GROUP, DSA_HEAD_GROUP * qb, IDX_DIM), BF16),
            pltpu.VMEM((ATTN_HEADS // DSA_HEAD_GROUP, qb, DSA_HEAD_GROUP * ATTN_HEAD_DIM), BF16),
            pltpu.VMEM((seq, DSA_HEAD_GROUP * qb), F32), pltpu.VMEM((seq, DSA_HEAD_GROUP * qb), F32),
        ],
        compiler_params=_params("parallel", "arbitrary"),
        name="dsa_attn",
    )(q, qi, wi_t, k.reshape(bsz, seq, -1), v_t, ki.reshape(bsz, seq, -1))

    tmo = 512
    return pl.pallas_call(
        _proj_res_kernel,
        grid=(t // tmo,),
        in_specs=[pl.BlockSpec((tmo, dq), row), _resident((dq, dm)), pl.BlockSpec((tmo, dm), row)],
        out_specs=pl.BlockSpec((tmo, dm), row),
        out_shape=jax.ShapeDtypeStruct((t, dm), F32),
        compiler_params=_params("parallel"),
        name="dsa_out",
    )(o, w_out.astype(BF16), x)


def kernel(x, ffn1_norm, ffn1_w_in, ffn1_w_out, mix_norm, ffn2_norm, ffn2_w_in, ffn2_w_out, ssd_in_proj, ssd_conv_w, ssd_conv_b, ssd_dt_bias, ssd_a_log, ssd_d, ssd_gate_norm, ssd_out_proj, s5_b_re, s5_b_im, s5_c_re, s5_c_im, s5_lam_re, s5_lam_im, s5_log_step, s5_d, s5_glu_w, s5_glu_b, dsa_in_proj, dsa_out_proj, final_norm):
    bsz, seq, dm = x.shape
    depth = ffn1_norm.shape[0]
    xt = x.reshape(bsz * seq, dm)
    for i in range(depth):
        j, kind = divmod(i, N_MIXERS)
        xt = _ffn_call(xt, ffn1_norm[i], ffn1_w_in[i].astype(BF16), ffn1_w_out[i].astype(BF16))
        if kind == 0:
            xt = _ssd_call(xt, bsz, seq, mix_norm[i], ssd_in_proj[j], ssd_conv_w[j], ssd_conv_b[j],
                           ssd_dt_bias[j], ssd_a_log[j], ssd_d[j], ssd_gate_norm[j], ssd_out_proj[j])
        elif kind == 1:
            xt = _s5_call(xt, bsz, seq, mix_norm[i], s5_b_re[j], s5_b_im[j], s5_c_re[j], s5_c_im[j],
                          s5_lam_re[j], s5_lam_im[j], s5_log_step[j], s5_d[j], s5_glu_w[j], s5_glu_b[j])
        else:
            xt = _dsa_call(xt, bsz, seq, mix_norm[i], dsa_in_proj[j], dsa_out_proj[j])
        post = final_norm if i == depth - 1 else None
        xt = _ffn_call(xt, ffn2_norm[i], ffn2_w_in[i].astype(BF16), ffn2_w_out[i].astype(BF16), post)
    return xt.reshape(bsz, seq, dm)
```

```python
import functools
import math

import jax
import jax.numpy as jnp
from jax import lax
from jax.experimental import pallas as pl
from jax.experimental.pallas import tpu as pltpu

F32 = jnp.float32
BF16 = jnp.bfloat16

NORM_EPS = 1e-6
N_MIXERS = 3

SSD_HEAD_DIM = 64
SSD_GROUPS = 8
SSD_HEADS_PER_GROUP = 4
SSD_HEADS = SSD_GROUPS * SSD_HEADS_PER_GROUP
SSD_STATE = 128
SSD_INNER = SSD_HEADS * SSD_HEAD_DIM
SSD_CONV = 4
SSD_CHUNK_MAX = 256
SSD_GROUP_WIDTH = SSD_HEADS_PER_GROUP * SSD_HEAD_DIM

S5_GROUP_WIDTH = 16
S5_STATE = 64
S5_SUB = 16

ATTN_HEADS = 16
ATTN_HEAD_DIM = 64
IDX_HEADS = 8
IDX_DIM = 64
TOPK_MAX = 256
QUERY_BLOCK = 128
ROPE_THETA = 10000.0
DSA_HEAD_GROUP = 4

LANES = 128
SUBLANES = 8
VMEM_LIMIT_BYTES = 56 * 1024 * 1024

_NT = (((1,), (1,)), ((), ()))


def _params(*semantics):
    return pltpu.CompilerParams(dimension_semantics=semantics, vmem_limit_bytes=VMEM_LIMIT_BYTES)


def _rms(x, g):
    ms = jnp.mean(x * x, axis=-1, keepdims=True)
    return x * lax.rsqrt(ms + NORM_EPS) * g


def _silu(x):
    return x * jax.nn.sigmoid(x)


def _softplus(x):
    return jnp.maximum(x, 0.0) + jnp.log1p(jnp.exp(-jnp.abs(x)))


def _resident(shape):
    zeros = (0,) * len(shape)
    return pl.BlockSpec(shape, lambda *_: zeros, pipeline_mode=pl.Buffered(1))


FFN_CHUNK = 256


def _ffn_kernel(x_ref, g_ref, win_ref, wout_ref, pg_ref, o_ref, *, post_norm):
    x = x_ref[...]
    h = _rms(x, g_ref[...]).astype(BF16)
    f = wout_ref.shape[0]
    acc = jnp.zeros(x.shape, F32)
    for j in range(f // FFN_CHUNK):
        lo = j * FFN_CHUNK
        gate = jnp.dot(h, win_ref[:, lo:lo + FFN_CHUNK], preferred_element_type=F32)
        up = jnp.dot(h, win_ref[:, f + lo:f + lo + FFN_CHUNK], preferred_element_type=F32)
        act = (_silu(gate) * up).astype(BF16)
        acc = acc + jnp.dot(act, wout_ref[lo:lo + FFN_CHUNK, :], preferred_element_type=F32)
    y = x + 0.5 * acc
    if post_norm:
        y = _rms(y, pg_ref[...])
    o_ref[...] = y


def _ffn_call(x, g, w_in, w_out, post_g=None):
    t, d = x.shape
    f = w_out.shape[0]
    tm = 1024
    assert t % tm == 0 and f % FFN_CHUNK == 0
    post_norm = post_g is not None
    pg = post_g if post_norm else g
    return pl.pallas_call(
        functools.partial(_ffn_kernel, post_norm=post_norm),
        grid=(t // tm,),
        in_specs=[
            pl.BlockSpec((tm, d), lambda i: (i, 0)),
            _resident((1, d)), _resident(w_in.shape), _resident(w_out.shape), _resident((1, d)),
        ],
        out_specs=pl.BlockSpec((tm, d), lambda i: (i, 0)),
        out_shape=jax.ShapeDtypeStruct((t, d), F32),
        compiler_params=_params("parallel"),
        name="ffn",
    )(x, g.reshape(1, d), w_in, w_out, pg.reshape(1, d))


def _split3(a):
    hi = a.astype(BF16)
    r1 = a - hi.astype(F32)
    mid = r1.astype(BF16)
    lo = (r1 - mid.astype(F32)).astype(BF16)
    return hi, mid, lo


def _ssd_kernel(x_ref, gmix_ref, wz_ref, wxbc_ref, wdt_ref, wdtt_ref, convw_ref, convb_ref,
                dtbc_ref, dtbr_ref, ac_ref, ar_ref, dvec_ref, gn_ref, wout_ref, selg_ref, selh_ref,
                o_ref,
                raw_scr, carry_scr, xs_scr, bm_scr, cm_scr, z_scr, yn_scr, dtc_scr, dtr_scr, state_scr,
                *, q, n_ch):
    pad = SUBLANES
    rows = n_ch * q
    gw = SSD_GROUP_WIDTH
    hd_dim = SSD_HEAD_DIM
    n_state = SSD_STATE
    hpg = SSD_HEADS_PER_GROUP

    @pl.when(pl.program_id(1) == 0)
    def _():
        state_scr[...] = jnp.zeros_like(state_scr)
        carry_scr[...] = jnp.zeros_like(carry_scr)

    x = x_ref[...]
    h = _rms(x, gmix_ref[...]).astype(BF16)
    cw = raw_scr.shape[1]
    n_x = SSD_INNER // cw
    n_b = SSD_GROUPS * n_state // cw
    n_cols = wxbc_ref.shape[1] // cw
    zw = wz_ref.shape[1] // n_cols
    for j in range(n_cols):
        sl = slice(j * cw, (j + 1) * cw)
        raw = raw_scr
        raw[0:pad, :] = carry_scr[:, sl]
        raw[pad:pad + rows, :] = jnp.dot(h, wxbc_ref[:, sl], preferred_element_type=F32)
        carry_scr[:, sl] = raw[rows:rows + pad, :]
        z_scr[:, j * zw:(j + 1) * zw] = jnp.dot(h, wz_ref[:, j * zw:(j + 1) * zw], preferred_element_type=F32)
        w = convw_ref[:, sl]
        acc = raw[pad - 3:pad - 3 + rows, :] * w[0:1]
        acc = acc + raw[pad - 2:pad - 2 + rows, :] * w[1:2]
        acc = acc + raw[pad - 1:pad - 1 + rows, :] * w[2:3]
        acc = acc + raw[pad:pad + rows, :] * w[3:4]
        act = _silu(acc + convb_ref[:, sl])
        if j < n_x:
            xs_scr[:, sl] = act
        elif j < n_x + n_b:
            bm_scr[:, (j - n_x) * cw:(j - n_x + 1) * cw] = act
        else:
            cm_scr[:, (j - n_x - n_b) * cw:(j - n_x - n_b + 1) * cw] = act

    dtc_scr[...] = _softplus(jnp.dot(h, wdt_ref[...], preferred_element_type=F32) + dtbc_ref[...])
    dt_rows = _softplus(lax.dot_general(wdtt_ref[...], h, _NT, preferred_element_type=F32) + dtbr_ref[...])
    for ci in range(n_ch):
        dtr_scr[ci] = dt_rows[:, ci * q:(ci + 1) * q]
    row = lax.broadcasted_iota(jnp.int32, (q, q), 0)
    col = lax.broadcasted_iota(jnp.int32, (q, q), 1)
    causal = col <= row
    tri_low = jnp.where(causal, 1.0, 0.0).astype(BF16)
    tri_up = jnp.where(row <= col, 1.0, 0.0).astype(BF16)

    def expand(a, sel):
        return sum(jnp.dot(p, sel, preferred_element_type=F32) for p in _split3(a))

    def chunk(ci, carry):
        r = pl.ds(pl.multiple_of(ci * q, q), q)
        dt_r = dtr_scr[ci]
        a_c = dtc_scr[r, :] * ac_ref[...]
        a_r = dt_r * ar_ref[...]
        cs_c = sum(jnp.dot(tri_low, p, preferred_element_type=F32) for p in _split3(a_c))
        cs_r = sum(jnp.dot(p, tri_up, preferred_element_type=F32) for p in _split3(a_r))
        e_c = jnp.exp(cs_c)
        cs_end = cs_r[:, q - 1:q]
        f_r = dt_r * jnp.exp(cs_end - cs_r)
        e_end = jnp.exp(cs_end)
        for g in range(SSD_GROUPS):
            bg = bm_scr[r, g * n_state:(g + 1) * n_state].astype(BF16)
            cg = cm_scr[r, g * n_state:(g + 1) * n_state].astype(BF16)
            cb = lax.dot_general(cg, bg, _NT, preferred_element_type=F32)
            xs_g = xs_scr[r, g * gw:(g + 1) * gw]
            xs_gb = xs_g.astype(BF16)
            xs_gt = xs_g.T
            st_g = state_scr[g]
            y_off = lax.dot_general(cg, st_g.astype(BF16), _NT, preferred_element_type=F32)
            cs_cols = expand(cs_c, selh_ref[g])
            ys, ws, decs = [], [], []
            for k in range(hpg):
                hd = g * hpg + k
                cs_col = cs_cols[:, k * LANES:(k + 1) * LANES]
                seg = jnp.concatenate([cs_col] * (q // LANES), axis=1) - cs_r[hd:hd + 1, :]
                decay = jnp.exp(jnp.where(causal, seg, -jnp.inf))
                m = (cb * decay * dt_r[hd:hd + 1, :]).astype(BF16)
                ys.append(jnp.dot(m, xs_gb[:, k * hd_dim:(k + 1) * hd_dim], preferred_element_type=F32))
                ws.append(xs_gt[k * hd_dim:(k + 1) * hd_dim, :] * f_r[hd:hd + 1, :])
                decs.append(jnp.broadcast_to(e_end[hd:hd + 1, :], (hd_dim, n_state)))
            wt = jnp.concatenate(ws, axis=0)
            upd = jnp.dot(wt.astype(BF16), bg, preferred_element_type=F32)
            state_scr[g] = st_g * jnp.concatenate(decs, axis=0) + upd
            y_g = jnp.concatenate(ys, axis=1) + y_off * expand(e_c, selg_ref[g])
            y_g = y_g + dvec_ref[:, g * gw:(g + 1) * gw] * xs_g
            yg = y_g * _silu(z_scr[r, g * gw:(g + 1) * gw])
            yg = yg * lax.rsqrt(jnp.mean(yg * yg, axis=-1, keepdims=True) + NORM_EPS)
            yn_scr[r, g * gw:(g + 1) * gw] = (yg * gn_ref[:, g * gw:(g + 1) * gw]).astype(BF16)
        return carry

    lax.fori_loop(0, n_ch, chunk, 0)
    o_ref[...] = x + jnp.dot(yn_scr[...], wout_ref[...], preferred_element_type=F32)


def _ssd_call(x, bsz, seq, g_mix, in_proj, conv_w, conv_b, dt_bias, a_log, d, gate_norm, out_proj):
    t, dm = x.shape
    q = math.gcd(seq, SSD_CHUNK_MAX)
    nc = seq // q
    conv_dim = SSD_INNER + 2 * SSD_GROUPS * SSD_STATE
    wz = in_proj[:, :SSD_INNER].astype(BF16)
    wxbc = in_proj[:, SSD_INNER:SSD_INNER + conv_dim].astype(BF16)
    wdt = in_proj[:, SSD_INNER + conv_dim:]
    wdt_c = jnp.pad(wdt, ((0, 0), (0, LANES - SSD_HEADS))).astype(BF16)
    wdt_r = wdt.T.astype(BF16)
    a = -jnp.exp(a_log.astype(F32))
    lane_pad = (0, LANES - SSD_HEADS)
    head_of = jnp.arange(LANES)
    grp_heads = jnp.arange(SSD_GROUPS) * SSD_HEADS_PER_GROUP
    args = (
        x, g_mix.reshape(1, dm), wz, wxbc, wdt_c, wdt_r, conv_w, conv_b.reshape(1, conv_dim),
        jnp.pad(dt_bias, lane_pad).reshape(1, LANES), dt_bias.reshape(SSD_HEADS, 1),
        jnp.pad(a, lane_pad).reshape(1, LANES), a.reshape(SSD_HEADS, 1),
        jnp.repeat(d, SSD_HEAD_DIM).reshape(1, SSD_INNER), gate_norm.reshape(1, SSD_INNER),
        out_proj.astype(BF16),
        (head_of[:, None] == (grp_heads[:, None, None] + jnp.arange(SSD_GROUP_WIDTH) // SSD_HEAD_DIM)).astype(BF16),
        (head_of[:, None] == (grp_heads[:, None, None] + jnp.arange(SSD_HEADS_PER_GROUP * LANES) // LANES)).astype(BF16),
    )
    n_ch = math.gcd(nc, 2)
    steps = nc // n_ch
    rows = n_ch * q
    x_spec = pl.BlockSpec((rows, dm), lambda b, c: (b * steps + c, 0))
    in_specs = [x_spec] + [_resident(a_.shape) for a_ in args[1:]]
    return pl.pallas_call(
        functools.partial(_ssd_kernel, q=q, n_ch=n_ch),
        grid=(bsz, steps),
        in_specs=in_specs,
        out_specs=x_spec,
        out_shape=jax.ShapeDtypeStruct((t, dm), F32),
        scratch_shapes=[
            pltpu.VMEM((rows + 2 * SUBLANES, 4 * LANES), F32),
            pltpu.VMEM((SUBLANES, conv_dim), F32),
            pltpu.VMEM((rows, SSD_INNER), F32),
            pltpu.VMEM((rows, SSD_GROUPS * SSD_STATE), F32),
            pltpu.VMEM((rows, SSD_GROUPS * SSD_STATE), F32),
            pltpu.VMEM((rows, SSD_INNER), F32),
            pltpu.VMEM((rows, SSD_INNER), BF16),
            pltpu.VMEM((rows, LANES), F32),
            pltpu.VMEM((n_ch, SSD_HEADS, q), F32),
            pltpu.VMEM((SSD_GROUPS, SSD_GROUP_WIDTH, SSD_STATE), F32),
        ],
        compiler_params=_params("arbitrary", "arbitrary"),
        name="ssd_mixer",
    )(*args)


S5_GB = LANES // S5_GROUP_WIDTH


def _s5_pack_kernel(x_ref, g_ref, o_ref, h_scr):
    h = _rms(x_ref[...], g_ref[...])
    rows = o_ref.shape[1]
    for gb in range(o_ref.shape[0]):
        h_scr[gb] = h[:, gb * LANES:(gb + 1) * LANES]
    for s in range(S5_SUB):
        for gb in range(o_ref.shape[0]):
            piece = h_scr[gb, pl.ds(s, rows, stride=S5_SUB), :]
            o_ref[gb, :, s * LANES:(s + 1) * LANES] = piece.astype(o_ref.dtype)


def _s5_scan_kernel(u_ref, mop_ref, gop_ref, cop_ref, lam_ref, y_ref, g_scr, hp_scr, *, n_bt, n_sub):
    n_half = g_scr.shape[0] // 2
    tile = 2 * LANES
    u = u_ref[...]
    g_in = jnp.dot(u, gop_ref[...], preferred_element_type=F32)
    for c in range(2 * n_half):
        g_scr[c] = g_in[:, c * LANES:(c + 1) * LANES]
    lam = lam_ref[...]
    lr = [jnp.broadcast_to(lam[0:1, c * LANES:(c + 1) * LANES], (n_bt, LANES)) for c in range(n_half)]
    li = [jnp.broadcast_to(lam[1:2, c * LANES:(c + 1) * LANES], (n_bt, LANES)) for c in range(n_half)]
    hr = [jnp.zeros((n_bt, LANES), F32)] * n_half
    hi = [jnp.zeros((n_bt, LANES), F32)] * n_half
    for j in range(n_sub):
        rows = pl.ds(j, n_bt, stride=n_sub)
        for c in range(n_half):
            hp_scr[c, rows, :] = hr[c]
            hp_scr[n_half + c, rows, :] = hi[c]
            gr, gi = g_scr[c, rows, :], g_scr[n_half + c, rows, :]
            hr[c], hi[c] = lr[c] * hr[c] - li[c] * hi[c] + gr, lr[c] * hi[c] + li[c] * hr[c] + gi
    h_in = jnp.concatenate([hp_scr[c] for c in range(2 * n_half)], axis=1).astype(BF16)
    y_inter = jnp.dot(h_in, cop_ref[...], preferred_element_type=F32)
    n_t = u.shape[1] // tile
    for tp in range(n_t):
        k_hi = (tp + 1) * tile
        y_intra = jnp.dot(u[:, :k_hi], mop_ref[(n_t - 1 - tp) * tile:, :], preferred_element_type=F32)
        y_ref[:, tp * tile:(tp + 1) * tile] = y_inter[:, tp * tile:(tp + 1) * tile] + y_intra


def _s5_operators(b_re, b_im, c_re, c_im, lam_re, lam_im, log_step):
    hp = lax.Precision.HIGHEST
    s, gb, w = S5_SUB, S5_GB, S5_GROUP_WIDTH
    n_g, n_p = lam_re.shape
    n_blk = n_g // gb
    lam = lax.complex(lam_re.astype(F32), lam_im.astype(F32))
    step = jnp.exp(log_step.astype(F32))[:, None]
    tau = jnp.arange(s + 1, dtype=F32)[:, None, None]
    lam_pow = jnp.exp((lam * step)[None] * tau)
    b_bar = ((lam_pow[1] - 1.0) / lam)[..., None] * lax.complex(b_re.astype(F32), b_im.astype(F32))
    cc = lax.complex(c_re.astype(F32), c_im.astype(F32))
    eye = jnp.eye(gb, dtype=F32)
    kern = jnp.einsum('gcp,tgp,gpd->tgcd', cc, lam_pow[:s], b_bar, precision=hp).real
    bd = jnp.einsum('tagcd,gh->atgdhc', kern.reshape(s, n_blk, gb, w, w), eye).reshape(n_blk, s, gb * w, gb * w)
    bd = jnp.concatenate([bd, jnp.zeros_like(bd[:, :1])], axis=1)
    two = jnp.arange(2)
    tile_lag = jnp.arange(s // 2 - 1, -1, -1)
    lag = 2 * tile_lag[:, None, None] + two[None, None, :] - two[None, :, None]
    mop = bd[:, jnp.where(lag < 0, s, lag)]
    mop = mop.transpose(0, 1, 2, 4, 3, 5).reshape(n_blk, s * gb * w, 2 * gb * w)
    ti = jnp.arange(s)
    gm = (lam_pow[s - 1 - ti][:, :, :, None] * b_bar[None]).reshape(s, n_blk, gb, n_p, w)
    gop = jnp.stack([jnp.einsum('sagpd,gh->asgdhp', part, eye) for part in (gm.real, gm.imag)], axis=4)
    gop = gop.reshape(n_blk, s * gb * w, 2 * gb * n_p)
    em = (cc[None] * lam_pow[1:s + 1][:, :, None, :]).reshape(s, n_blk, gb, w, n_p)
    cop = jnp.stack([jnp.einsum('tagcp,gh->agpthc', part, eye) for part in (em.real, -em.imag)], axis=1)
    cop = cop.reshape(n_blk, 2 * gb * n_p, s * gb * w)
    lam_s = lam_pow[s].reshape(n_blk, gb * n_p)
    lam_blk = jnp.stack([lam_s.real, lam_s.imag], axis=1)
    return mop.astype(BF16), gop.astype(BF16), cop.astype(BF16), lam_blk


def _glu_kernel(x_ref, y_ref, g_ref, d_ref, w_ref, b_ref, o_ref, y_scr):
    x = x_ref[...]
    d_model = x.shape[1]
    rows = y_ref.shape[1]
    for s in range(S5_SUB):
        for gb in range(y_ref.shape[0]):
            y_scr[gb, pl.ds(s, rows, stride=S5_SUB), :] = y_ref[gb, :, s * LANES:(s + 1) * LANES]
    y = jnp.concatenate([y_scr[gb] for gb in range(y_ref.shape[0])], axis=1)
    h = _rms(x, g_ref[...])
    act = jax.nn.gelu(y + d_ref[...] * h).astype(BF16)
    vg = jnp.dot(act, w_ref[...], preferred_element_type=F32) + b_ref[...]
    o_ref[...] = x + vg[:, :d_model] * jax.nn.sigmoid(vg[:, d_model:])


def _s5_call(x, bsz, seq, g_mix, b_re, b_im, c_re, c_im, lam_re, lam_im, log_step, d, glu_w, glu_b):
    t, dm = x.shape
    s = S5_SUB
    n_blk = dm // LANES
    n_sub = seq // s
    rows = t // s
    width = s * LANES
    state_w = 2 * S5_GB * S5_STATE
    n_bt = 4
    tm = 512
    assert seq % s == 0 and bsz % n_bt == 0 and t % tm == 0
    mop, gop, cop, lam_blk = _s5_operators(b_re, b_im, c_re, c_im, lam_re, lam_im, log_step)
    tok = lambda i: (i, 0)
    packed = lambda i: (0, i, 0)
    u = pl.pallas_call(
        _s5_pack_kernel,
        grid=(t // tm,),
        in_specs=[pl.BlockSpec((tm, dm), tok), _resident((1, dm))],
        out_specs=pl.BlockSpec((n_blk, tm // s, width), packed),
        out_shape=jax.ShapeDtypeStruct((n_blk, rows, width), BF16),
        scratch_shapes=[pltpu.VMEM((n_blk, tm, LANES), F32)],
        compiler_params=_params("parallel"),
        name="s5_pack",
    )(x, g_mix.reshape(1, dm))
    rt = n_bt * n_sub
    per_blk = lambda a, r: (a, 0, 0)
    once = pl.Buffered(1)
    y = pl.pallas_call(
        functools.partial(_s5_scan_kernel, n_bt=n_bt, n_sub=n_sub),
        grid=(n_blk, rows // rt),
        in_specs=[
            pl.BlockSpec((None, rt, width), lambda a, r: (a, r, 0)),
            pl.BlockSpec((None, width, 2 * LANES), per_blk, pipeline_mode=once),
            pl.BlockSpec((None, width, state_w), per_blk, pipeline_mode=once),
            pl.BlockSpec((None, state_w, width), per_blk, pipeline_mode=once),
            pl.BlockSpec((None, 2, state_w // 2), per_blk),
        ],
        out_specs=pl.BlockSpec((None, rt, width), lambda a, r: (a, r, 0)),
        out_shape=jax.ShapeDtypeStruct((n_blk, rows, width), F32),
        scratch_shapes=[pltpu.VMEM((state_w // LANES, rt, LANES), F32)] * 2,
        compiler_params=_params("arbitrary", "arbitrary"),
        name="s5_scan",
    )(u, mop, gop, cop, lam_blk)
    return pl.pallas_call(
        _glu_kernel,
        grid=(t // tm,),
        in_specs=[
            pl.BlockSpec((tm, dm), tok),
            pl.BlockSpec((n_blk, tm // s, width), packed),
            _resident((1, dm)), _resident((1, dm)), _resident((dm, 2 * dm)), _resident((1, 2 * dm)),
        ],
        out_specs=pl.BlockSpec((tm, dm), tok),
        out_shape=jax.ShapeDtypeStruct((t, dm), F32),
        scratch_shapes=[pltpu.VMEM((n_blk, tm, LANES), F32)],
        compiler_params=_params("parallel"),
        name="s5_glu",
    )(x, y, g_mix.reshape(1, dm), d.reshape(1, dm), glu_w.astype(BF16), glu_b.reshape(1, 2 * dm))


def _rope128(tile, cos, sin_hi, sin_lo):
    return tile * cos + pltpu.roll(tile, 32, 1) * sin_hi + pltpu.roll(tile, LANES - 32, 1) * sin_lo


def _dsa_proj_kernel(x_ref, g_ref, wq_ref, wqi_ref, wsm_ref, tab_ref, tabh_ref,
                     q_ref, qi_ref, k_ref, v_ref, ki_ref, wi_ref):
    h = _rms(x_ref[...], g_ref[...]).astype(BF16)
    cos, s_hi, s_lo = tab_ref[0], tab_ref[1], tab_ref[2]
    q = jnp.dot(h, wq_ref[...], preferred_element_type=F32)
    scale = ATTN_HEAD_DIM ** -0.5
    for j in range(q.shape[1] // LANES):
        sl = slice(j * LANES, (j + 1) * LANES)
        q_ref[:, sl] = (_rope128(q[:, sl], cos, s_hi, s_lo) * scale).astype(BF16)
    qi = jnp.dot(h, wqi_ref[...], preferred_element_type=F32)
    for j in range(qi.shape[1] // LANES):
        sl = slice(j * LANES, (j + 1) * LANES)
        qi_ref[:, sl] = _rope128(qi[:, sl], cos, s_hi, s_lo).astype(BF16)
    sm = jnp.dot(h, wsm_ref[...], preferred_element_type=F32)
    kv = _rope128(sm[:, :LANES], tabh_ref[0], tabh_ref[1], tabh_ref[2])
    kw = _rope128(sm[:, LANES:], tabh_ref[3], tabh_ref[1], tabh_ref[2])
    k_ref[...] = kv[:, :ATTN_HEAD_DIM].astype(BF16)
    v_ref[...] = kv[:, ATTN_HEAD_DIM:].astype(BF16)
    ki_ref[...] = kw[:, :IDX_DIM].astype(BF16)
    wi_ref[...] = kw


def _reduce_rows(x, op):
    slab = 8 * SUBLANES
    part = op(x.reshape(x.shape[0] // slab, slab, x.shape[1]), axis=0)
    return op(part, axis=0, keepdims=True)


def _dsa_attn_kernel(q_ref, qi_ref, wi_ref, k_ref, vt_ref, ki_ref, o_ref,
                     key_scr, bias_scr, qs_scr, qis_scr, og_scr, lga_scr, lgb_scr, *, top_k, span_ends):
    n = pl.program_id(1)
    qb = key_scr.shape[1]
    hg, dh = DSA_HEAD_GROUP, ATTN_HEAD_DIM
    for g in range(qs_scr.shape[0]):
        qs_scr[g] = jnp.concatenate([q_ref[:, (g * hg + j) * dh:(g * hg + j + 1) * dh] for j in range(hg)], axis=0)
    for g in range(qis_scr.shape[0]):
        qis_scr[g] = jnp.concatenate(
            [qi_ref[:, (g * hg + j) * IDX_DIM:(g * hg + j + 1) * IDX_DIM] for j in range(hg)], axis=0)
    for lo, hi in zip((0,) + span_ends[:-1], span_ends):
        @pl.when((n >= lo) & (n < hi))
        def _(hi=hi):
            _dsa_attn_block(wi_ref, k_ref, vt_ref, ki_ref, key_scr, bias_scr, qs_scr, qis_scr, og_scr,
                            lga_scr, lgb_scr,
                            top_k=top_k, seq=hi * qb)
    for g in range(og_scr.shape[0]):
        o_ref[:, g * hg * dh:(g + 1) * hg * dh] = og_scr[g]


def _dsa_attn_block(wi_ref, k_ref, vt_ref, ki_ref, key_full, bias_full, qs_scr, qis_scr, og_scr,
                    lga_scr, lgb_scr, *, top_k, seq):
    n = pl.program_id(1)
    qb = key_full.shape[1]
    key_scr = key_full.at[0:seq, :]
    bias_scr = bias_full.at[0:seq, :]
    kib = ki_ref[0:seq, :]
    wi = wi_ref[...]
    hgi = DSA_HEAD_GROUP
    idx = jnp.zeros((seq, qb), F32)
    for g in range(qis_scr.shape[0]):
        sc = lax.dot_general(kib, qis_scr[g], _NT, preferred_element_type=F32)
        for j in range(hgi):
            hh = g * hgi + j
            idx = idx + jnp.maximum(sc[:, j * qb:(j + 1) * qb], 0.0) * wi[hh:hh + 1, :]
    t_pos = n * qb + lax.broadcasted_iota(jnp.int32, (1, qb), 1)
    s_pos = lax.broadcasted_iota(jnp.int32, (seq, 1), 0)
    causal = s_pos <= t_pos
    idx = jnp.where(causal, idx, -jnp.inf)
    bits = lax.bitcast_convert_type(idx, jnp.int32)
    key_scr[...] = jnp.where(bits < 0, bits ^ jnp.int32(0x7FFFFFFF), bits)

    def count(pred):
        return _reduce_rows(jnp.where(pred, 1.0, 0.0), jnp.sum)

    int_min = jnp.int32(-2 ** 31)
    kf = float(top_k)
    thr0 = jnp.where(count(key_scr[...] >= 0) >= kf, jnp.int32(0), int_min)

    def thr_step(i, thr):
        cand = thr + lax.shift_left(jnp.int32(1), jnp.int32(30) - i)
        return jnp.where(count(key_scr[...] >= cand) >= kf, cand, thr)

    thr = lax.fori_loop(0, 31, thr_step, thr0)
    keys = key_scr[...]
    n_ge = count(keys >= thr)

    def tie_cut():
        need = kf - count(key_scr[...] > thr)
        n_bits = (seq - 1).bit_length()

        def cut_step(i, cut):
            cand = cut + lax.shift_left(jnp.int32(1), jnp.int32(n_bits - 1) - i)
            return jnp.where(count((key_scr[...] == thr) & (s_pos < cand)) < need, cand, cut)

        return lax.fori_loop(0, n_bits, cut_step, jnp.zeros((1, qb), jnp.int32))

    cut = lax.cond(jnp.max(n_ge) > kf, tie_cut, lambda: jnp.full((1, qb), seq, jnp.int32))
    sel = (keys > thr) | ((keys == thr) & (s_pos <= cut))
    bias_scr[...] = jnp.where(sel & causal, 0.0, -jnp.inf)

    hg = DSA_HEAD_GROUP

    n_grp = og_scr.shape[0]
    lg_even = lga_scr.at[0:seq, :]
    lg_odd = lgb_scr.at[0:seq, :]

    def scores(g):
        return lax.dot_general(k_ref[0:seq, :], qs_scr[g], _NT, preferred_element_type=F32)

    def finish(g, lg):
        lg = lg + jnp.concatenate([bias_scr[...]] * hg, axis=1)
        p = jnp.exp((lg - _reduce_rows(lg, jnp.max)).astype(BF16))
        ov = jnp.dot(vt_ref[:, 0:seq], p, preferred_element_type=F32)
        og = ov[0:ATTN_HEAD_DIM, :] / ov[ATTN_HEAD_DIM:ATTN_HEAD_DIM + 1, :]
        pairs = []
        for j in range(0, hg, 2):
            pair = jnp.concatenate([og[:, j * qb:(j + 1) * qb], og[:, (j + 1) * qb:(j + 2) * qb]], axis=0)
            pairs.append(pair.T)
        og_scr[g] = jnp.concatenate(pairs, axis=1).astype(BF16)

    lg_even[...] = scores(0)

    def group_pair(i, carry):
        g = 2 * i
        lg_odd[...] = scores(g + 1)
        finish(g, lg_even[...])
        lg_even[...] = scores(jnp.minimum(g + 2, n_grp - 1))
        finish(g + 1, lg_odd[...])
        return carry

    lax.fori_loop(0, n_grp // 2, group_pair, 0)


def _proj_res_kernel(a_ref, w_ref, x_ref, o_ref):
    o_ref[...] = x_ref[...] + jnp.dot(a_ref[...], w_ref[...], preferred_element_type=F32)


def _rope_tables(seq):
    half = ATTN_HEAD_DIM // 2
    inv = ROPE_THETA ** (-jnp.arange(half, dtype=F32) / half)
    ang = jnp.arange(seq, dtype=F32)[:, None] * inv
    cos, sin = jnp.cos(ang), jnp.sin(ang)
    zero, one = jnp.zeros_like(cos), jnp.ones_like(cos)
    cos64 = jnp.concatenate([cos, cos], axis=1)
    hi64 = jnp.concatenate([zero, sin], axis=1)
    lo64 = jnp.concatenate([-sin, zero], axis=1)
    full = jnp.stack([jnp.tile(t, (1, 2)) for t in (cos64, hi64, lo64)])
    ident = jnp.concatenate([one, one], axis=1)
    zero64 = jnp.concatenate([zero, zero], axis=1)
    wscale = ident * (IDX_HEADS ** -0.5 * IDX_DIM ** -0.5)
    halfs = jnp.stack([jnp.concatenate([cos64, ident], axis=1), jnp.concatenate([hi64, zero64], axis=1),
                       jnp.concatenate([lo64, zero64], axis=1), jnp.concatenate([cos64, wscale], axis=1)])
    return full, halfs


def _dsa_call(x, bsz, seq, g_mix, w_in, w_out):
    t, dm = x.shape
    dq = ATTN_HEADS * ATTN_HEAD_DIM
    dqi = IDX_HEADS * IDX_DIM
    o_k, o_v, o_qi = dq, dq + ATTN_HEAD_DIM, dq + 2 * ATTN_HEAD_DIM
    o_ki = o_qi + dqi
    o_wi = o_ki + IDX_DIM
    wq = w_in[:, :o_k].astype(BF16)
    wqi = w_in[:, o_qi:o_ki].astype(BF16)
    wsm = jnp.concatenate([w_in[:, o_k:o_qi], w_in[:, o_ki:],
                           jnp.zeros((dm, LANES - IDX_DIM - IDX_HEADS), w_in.dtype)], axis=1).astype(BF16)
    full, halfs = _rope_tables(seq)
    tm = 256
    nt = seq // tm
    row = lambda i: (i, 0)
    tab = lambda i: (0, i % nt, 0)
    q, qi, k, v, ki, wi = pl.pallas_call(
        _dsa_proj_kernel,
        grid=(t // tm,),
        in_specs=[
            pl.BlockSpec((tm, dm), row), _resident((1, dm)), _resident(wq.shape), _resident(wqi.shape),
            _resident(wsm.shape),
            pl.BlockSpec((3, tm, LANES), tab), pl.BlockSpec((4, tm, LANES), tab),
        ],
        out_specs=[
            pl.BlockSpec((tm, dq), row), pl.BlockSpec((tm, dqi), row),
            pl.BlockSpec((tm, ATTN_HEAD_DIM), row), pl.BlockSpec((tm, ATTN_HEAD_DIM), row),
            pl.BlockSpec((tm, IDX_DIM), row), pl.BlockSpec((tm, LANES), row),
        ],
        out_shape=[
            jax.ShapeDtypeStruct((t, dq), BF16), jax.ShapeDtypeStruct((t, dqi), BF16),
            jax.ShapeDtypeStruct((t, ATTN_HEAD_DIM), BF16), jax.ShapeDtypeStruct((t, ATTN_HEAD_DIM), BF16),
            jax.ShapeDtypeStruct((t, IDX_DIM), BF16), jax.ShapeDtypeStruct((t, LANES), F32),
        ],
        compiler_params=_params("parallel"),
        name="dsa_proj",
    )(x, g_mix.reshape(1, dm), wq, wqi, wsm, full, halfs)

    qb = QUERY_BLOCK
    nb = seq // qb
    top_k = min(TOPK_MAX, seq // 4)
    blk = lambda b, n: (b * nb + n, 0)
    per_b = lambda b, n: (b, 0, 0)
    span_ends = tuple(sorted({e for e in (nb // 8, nb // 4, 3 * nb // 8, nb // 2, 3 * nb // 4, nb) if e > 0}))
    wi_t = wi[:, IDX_DIM:IDX_DIM + IDX_HEADS].reshape(bsz, seq, IDX_HEADS).transpose(0, 2, 1)
    v_t = v.reshape(bsz, seq, ATTN_HEAD_DIM).transpose(0, 2, 1)
    ones_rows = jnp.zeros((bsz, 2 * SUBLANES, seq), BF16).at[:, 0, :].set(1.0)
    v_t = jnp.concatenate([v_t, ones_rows], axis=1)
    o = pl.pallas_call(
        functools.partial(_dsa_attn_kernel, top_k=top_k, span_ends=span_ends),
        grid=(bsz, nb),
        in_specs=[
            pl.BlockSpec((qb, dq), blk), pl.BlockSpec((qb, dqi), blk),
            pl.BlockSpec((None, IDX_HEADS, qb), lambda b, n: (b, 0, n)),
            pl.BlockSpec((None, seq, ATTN_HEAD_DIM), per_b),
            pl.BlockSpec((None, ATTN_HEAD_DIM + 2 * SUBLANES, seq), per_b),
            pl.BlockSpec((None, seq, IDX_DIM), per_b),
        ],
        out_specs=pl.BlockSpec((qb, dq), blk),
        out_shape=jax.ShapeDtypeStruct((t, dq), BF16),
        scratch_shapes=[
            pltpu.VMEM((seq, qb), jnp.int32), pltpu.VMEM((seq, qb), F32),
            pltpu.VMEM((ATTN_HEADS // DSA_HEAD_GROUP, DSA_HEAD_GROUP * qb, ATTN_HEAD_DIM), BF16),
            pltpu.VMEM((IDX_HEADS // DSA_HEAD_GROUP, DSA_HEAD_GROUP * qb, IDX_DIM), BF16),
            pltpu.VMEM((ATTN_HEADS // DSA_HEAD_GROUP, qb, DSA_HEAD_GROUP * ATTN_HEAD_DIM), BF16),
            pltpu.VMEM((seq, DSA_HEAD_GROUP * qb), F32), pltpu.VMEM((seq, DSA_HEAD_GROUP * qb), F32),
        ],
        compiler_params=_params("parallel", "arbitrary"),
        name="dsa_attn",
    )(q, qi, wi_t, k.reshape(bsz, seq, -1), v_t, ki.reshape(bsz, seq, -1))

    tmo = 512
    return pl.pallas_call(
        _proj_res_kernel,
        grid=(t // tmo,),
        in_specs=[pl.BlockSpec((tmo, dq), row), _resident((dq, dm)), pl.BlockSpec((tmo, dm), row)],
        out_specs=pl.BlockSpec((tmo, dm), row),
        out_shape=jax.ShapeDtypeStruct((t, dm), F32),
        compiler_params=_params("parallel"),
        name="dsa_out",
    )(o, w_out.astype(BF16), x)


def kernel(x, ffn1_norm, ffn1_w_in, ffn1_w_out, mix_norm, ffn2_norm, ffn2_w_in, ffn2_w_out, ssd_in_proj, ssd_conv_w, ssd_conv_b, ssd_dt_bias, ssd_a_log, ssd_d, ssd_gate_norm, ssd_out_proj, s5_b_re, s5_b_im, s5_c_re, s5_c_im, s5_lam_re, s5_lam_im, s5_log_step, s5_d, s5_glu_w, s5_glu_b, dsa_in_proj, dsa_out_proj, final_norm):
    bsz, seq, dm = x.shape
    depth = ffn1_norm.shape[0]
    xt = x.reshape(bsz * seq, dm)
    for i in range(depth):
        j, kind = divmod(i, N_MIXERS)
        xt = _ffn_call(xt, ffn1_norm[i], ffn1_w_in[i].astype(BF16), ffn1_w_out[i].astype(BF16))
        if kind == 0:
            xt = _ssd_call(xt, bsz, seq, mix_norm[i], ssd_in_proj[j], ssd_conv_w[j], ssd_conv_b[j],
                           ssd_dt_bias[j], ssd_a_log[j], ssd_d[j], ssd_gate_norm[j], ssd_out_proj[j])
        elif kind == 1:
            xt = _s5_call(xt, bsz, seq, mix_norm[i], s5_b_re[j], s5_b_im[j], s5_c_re[j], s5_c_im[j],
                          s5_lam_re[j], s5_lam_im[j], s5_log_step[j], s5_d[j], s5_glu_w[j], s5_glu_b[j])
        else:
            xt = _dsa_call(xt, bsz, seq, mix_norm[i], dsa_in_proj[j], dsa_out_proj[j])
        post = final_norm if i == depth - 1 else None
        xt = _ffn_call(xt, ffn2_norm[i], ffn2_w_in[i].astype(BF16), ffn2_w_out[i].astype(BF16), post)
    return xt.reshape(bsz, seq, dm)
```

```python
import functools
import math

import jax
import jax.numpy as jnp
from jax import lax
from jax.experimental import pallas as pl
from jax.experimental.pallas import tpu as pltpu

F32 = jnp.float32
BF16 = jnp.bfloat16

NORM_EPS = 1e-6
N_MIXERS = 3

SSD_HEAD_DIM = 64
SSD_GROUPS = 8
SSD_HEADS_PER_GROUP = 4
SSD_HEADS = SSD_GROUPS * SSD_HEADS_PER_GROUP
SSD_STATE = 128
SSD_INNER = SSD_HEADS * SSD_HEAD_DIM
SSD_CONV = 4
SSD_CHUNK_MAX = 256
SSD_GROUP_WIDTH = SSD_HEADS_PER_GROUP * SSD_HEAD_DIM

S5_GROUP_WIDTH = 16
S5_STATE = 64
S5_SUB = 16

ATTN_HEADS = 16
ATTN_HEAD_DIM = 64
IDX_HEADS = 8
IDX_DIM = 64
TOPK_MAX = 256
QUERY_BLOCK = 128
ROPE_THETA = 10000.0
DSA_HEAD_GROUP = 4

LANES = 128
SUBLANES = 8
VMEM_LIMIT_BYTES = 56 * 1024 * 1024

_NT = (((1,), (1,)), ((), ()))


def _params(*semantics):
    return pltpu.CompilerParams(dimension_semantics=semantics, vmem_limit_bytes=VMEM_LIMIT_BYTES)


def _rms(x, g):
    ms = jnp.mean(x * x, axis=-1, keepdims=True)
    return x * lax.rsqrt(ms + NORM_EPS) * g


def _silu(x):
    return x * jax.nn.sigmoid(x)


def _softplus(x):
    return jnp.maximum(x, 0.0) + jnp.log1p(jnp.exp(-jnp.abs(x)))


def _resident(shape):
    zeros = (0,) * len(shape)
    return pl.BlockSpec(shape, lambda *_: zeros, pipeline_mode=pl.Buffered(1))


FFN_CHUNK = 256


def _ffn_kernel(x_ref, g_ref, win_ref, wout_ref, pg_ref, o_ref, *, post_norm):
    x = x_ref[...]
    h = _rms(x, g_ref[...]).astype(BF16)
    f = wout_ref.shape[0]
    acc = jnp.zeros(x.shape, F32)
    for j in range(f // FFN_CHUNK):
        lo = j * FFN_CHUNK
        gate = jnp.dot(h, win_ref[:, lo:lo + FFN_CHUNK], preferred_element_type=F32)
        up = jnp.dot(h, win_ref[:, f + lo:f + lo + FFN_CHUNK], preferred_element_type=F32)
        act = (_silu(gate) * up).astype(BF16)
        acc = acc + jnp.dot(act, wout_ref[lo:lo + FFN_CHUNK, :], preferred_element_type=F32)
    y = x + 0.5 * acc
    if post_norm:
        y = _rms(y, pg_ref[...])
    o_ref[...] = y


def _ffn_call(x, g, w_in, w_out, post_g=None):
    t, d = x.shape
    f = w_out.shape[0]
    tm = 1024
    assert t % tm == 0 and f % FFN_CHUNK == 0
    post_norm = post_g is not None
    pg = post_g if post_norm else g
    return pl.pallas_call(
        functools.partial(_ffn_kernel, post_norm=post_norm),
        grid=(t // tm,),
        in_specs=[
            pl.BlockSpec((tm, d), lambda i: (i, 0)),
            _resident((1, d)), _resident(w_in.shape), _resident(w_out.shape), _resident((1, d)),
        ],
        out_specs=pl.BlockSpec((tm, d), lambda i: (i, 0)),
        out_shape=jax.ShapeDtypeStruct((t, d), F32),
        compiler_params=_params("parallel"),
        name="ffn",
    )(x, g.reshape(1, d), w_in, w_out, pg.reshape(1, d))


def _split3(a):
    hi = a.astype(BF16)
    r1 = a - hi.astype(F32)
    mid = r1.astype(BF16)
    lo = (r1 - mid.astype(F32)).astype(BF16)
    return hi, mid, lo


def _ssd_kernel(x_ref, gmix_ref, wz_ref, wxbc_ref, wdt_ref, wdtt_ref, convw_ref, convb_ref,
                dtbc_ref, dtbr_ref, ac_ref, ar_ref, dvec_ref, gn_ref, wout_ref, selg_ref, selh_ref,
                o_ref,
                raw_scr, carry_scr, xs_scr, bm_scr, cm_scr, z_scr, yn_scr, dtc_scr, dtr_scr, state_scr,
                *, q, n_ch):
    pad = SUBLANES
    rows = n_ch * q
    gw = SSD_GROUP_WIDTH
    hd_dim = SSD_HEAD_DIM
    n_state = SSD_STATE
    hpg = SSD_HEADS_PER_GROUP

    @pl.when(pl.program_id(1) == 0)
    def _():
        state_scr[...] = jnp.zeros_like(state_scr)
        carry_scr[...] = jnp.zeros_like(carry_scr)

    x = x_ref[...]
    h = _rms(x, gmix_ref[...]).astype(BF16)
    cw = raw_scr.shape[1]
    n_x = SSD_INNER // cw
    n_b = SSD_GROUPS * n_state // cw
    n_cols = wxbc_ref.shape[1] // cw
    zw = wz_ref.shape[1] // n_cols
    for j in range(n_cols):
        sl = slice(j * cw, (j + 1) * cw)
        raw = raw_scr
        raw[0:pad, :] = carry_scr[:, sl]
        raw[pad:pad + rows, :] = jnp.dot(h, wxbc_ref[:, sl], preferred_element_type=F32)
        carry_scr[:, sl] = raw[rows:rows + pad, :]
        z_scr[:, j * zw:(j + 1) * zw] = jnp.dot(h, wz_ref[:, j * zw:(j + 1) * zw], preferred_element_type=F32)
        w = convw_ref[:, sl]
        acc = raw[pad - 3:pad - 3 + rows, :] * w[0:1]
        acc = acc + raw[pad - 2:pad - 2 + rows, :] * w[1:2]
        acc = acc + raw[pad - 1:pad - 1 + rows, :] * w[2:3]
        acc = acc + raw[pad:pad + rows, :] * w[3:4]
        act = _silu(acc + convb_ref[:, sl])
        if j < n_x:
            xs_scr[:, sl] = act
        elif j < n_x + n_b:
            bm_scr[:, (j - n_x) * cw:(j - n_x + 1) * cw] = act
        else:
            cm_scr[:, (j - n_x - n_b) * cw:(j - n_x - n_b + 1) * cw] = act

    dtc_scr[...] = _softplus(jnp.dot(h, wdt_ref[...], preferred_element_type=F32) + dtbc_ref[...])
    dt_rows = _softplus(lax.dot_general(wdtt_ref[...], h, _NT, preferred_element_type=F32) + dtbr_ref[...])
    for ci in range(n_ch):
        dtr_scr[ci] = dt_rows[:, ci * q:(ci + 1) * q]
    row = lax.broadcasted_iota(jnp.int32, (q, q), 0)
    col = lax.broadcasted_iota(jnp.int32, (q, q), 1)
    causal = col <= row
    tri_low = jnp.where(causal, 1.0, 0.0).astype(BF16)
    tri_up = jnp.where(row <= col, 1.0, 0.0).astype(BF16)

    def expand(a, sel):
        return sum(jnp.dot(p, sel, preferred_element_type=F32) for p in _split3(a))

    def chunk(ci, carry):
        r = pl.ds(pl.multiple_of(ci * q, q), q)
        dt_r = dtr_scr[ci]
        a_c = dtc_scr[r, :] * ac_ref[...]
        a_r = dt_r * ar_ref[...]
        cs_c = sum(jnp.dot(tri_low, p, preferred_element_type=F32) for p in _split3(a_c))
        cs_r = sum(jnp.dot(p, tri_up, preferred_element_type=F32) for p in _split3(a_r))
        e_c = jnp.exp(cs_c)
        cs_end = cs_r[:, q - 1:q]
        f_r = dt_r * jnp.exp(cs_end - cs_r)
        e_end = jnp.exp(cs_end)
        for g in range(SSD_GROUPS):
            bg = bm_scr[r, g * n_state:(g + 1) * n_state].astype(BF16)
            cg = cm_scr[r, g * n_state:(g + 1) * n_state].astype(BF16)
            cb = lax.dot_general(cg, bg, _NT, preferred_element_type=F32)
            xs_g = xs_scr[r, g * gw:(g + 1) * gw]
            xs_gb = xs_g.astype(BF16)
            xs_gt = xs_g.T
            st_g = state_scr[g]
            y_off = lax.dot_general(cg, st_g.astype(BF16), _NT, preferred_element_type=F32)
            cs_cols = expand(cs_c, selh_ref[g])
            ys, ws, decs = [], [], []
            for k in range(hpg):
                hd = g * hpg + k
                cs_col = cs_cols[:, k * LANES:(k + 1) * LANES]
                seg = jnp.concatenate([cs_col] * (q // LANES), axis=1) - cs_r[hd:hd + 1, :]
                decay = jnp.exp(jnp.where(causal, seg, -jnp.inf))
                m = (cb * decay * dt_r[hd:hd + 1, :]).astype(BF16)
                ys.append(jnp.dot(m, xs_gb[:, k * hd_dim:(k + 1) * hd_dim], preferred_element_type=F32))
                ws.append(xs_gt[k * hd_dim:(k + 1) * hd_dim, :] * f_r[hd:hd + 1, :])
                decs.append(jnp.broadcast_to(e_end[hd:hd + 1, :], (hd_dim, n_state)))
            wt = jnp.concatenate(ws, axis=0)
            upd = jnp.dot(wt.astype(BF16), bg, preferred_element_type=F32)
            state_scr[g] = st_g * jnp.concatenate(decs, axis=0) + upd
            y_g = jnp.concatenate(ys, axis=1) + y_off * expand(e_c, selg_ref[g])
            y_g = y_g + dvec_ref[:, g * gw:(g + 1) * gw] * xs_g
            yg = y_g * _silu(z_scr[r, g * gw:(g + 1) * gw])
            yg = yg * lax.rsqrt(jnp.mean(yg * yg, axis=-1, keepdims=True) + NORM_EPS)
            yn_scr[r, g * gw:(g + 1) * gw] = (yg * gn_ref[:, g * gw:(g + 1) * gw]).astype(BF16)
        o_ref[r, :] = x_ref[r, :] + jnp.dot(yn_scr[r, :], wout_ref[...], preferred_element_type=F32)
        return carry

    lax.fori_loop(0, n_ch, chunk, 0)


def _ssd_call(x, bsz, seq, g_mix, in_proj, conv_w, conv_b, dt_bias, a_log, d, gate_norm, out_proj):
    t, dm = x.shape
    q = math.gcd(seq, SSD_CHUNK_MAX)
    nc = seq // q
    conv_dim = SSD_INNER + 2 * SSD_GROUPS * SSD_STATE
    wz = in_proj[:, :SSD_INNER].astype(BF16)
    wxbc = in_proj[:, SSD_INNER:SSD_INNER + conv_dim].astype(BF16)
    wdt = in_proj[:, SSD_INNER + conv_dim:]
    wdt_c = jnp.pad(wdt, ((0, 0), (0, LANES - SSD_HEADS))).astype(BF16)
    wdt_r = wdt.T.astype(BF16)
    a = -jnp.exp(a_log.astype(F32))
    lane_pad = (0, LANES - SSD_HEADS)
    head_of = jnp.arange(LANES)
    grp_heads = jnp.arange(SSD_GROUPS) * SSD_HEADS_PER_GROUP
    args = (
        x, g_mix.reshape(1, dm), wz, wxbc, wdt_c, wdt_r, conv_w, conv_b.reshape(1, conv_dim),
        jnp.pad(dt_bias, lane_pad).reshape(1, LANES), dt_bias.reshape(SSD_HEADS, 1),
        jnp.pad(a, lane_pad).reshape(1, LANES), a.reshape(SSD_HEADS, 1),
        jnp.repeat(d, SSD_HEAD_DIM).reshape(1, SSD_INNER), gate_norm.reshape(1, SSD_INNER),
        out_proj.astype(BF16),
        (head_of[:, None] == (grp_heads[:, None, None] + jnp.arange(SSD_GROUP_WIDTH) // SSD_HEAD_DIM)).astype(BF16),
        (head_of[:, None] == (grp_heads[:, None, None] + jnp.arange(SSD_HEADS_PER_GROUP * LANES) // LANES)).astype(BF16),
    )
    n_ch = math.gcd(nc, 2)
    steps = nc // n_ch
    rows = n_ch * q
    x_spec = pl.BlockSpec((rows, dm), lambda b, c: (b * steps + c, 0))
    in_specs = [x_spec] + [_resident(a_.shape) for a_ in args[1:]]
    return pl.pallas_call(
        functools.partial(_ssd_kernel, q=q, n_ch=n_ch),
        grid=(bsz, steps),
        in_specs=in_specs,
        out_specs=x_spec,
        out_shape=jax.ShapeDtypeStruct((t, dm), F32),
        scratch_shapes=[
            pltpu.VMEM((rows + 2 * SUBLANES, 4 * LANES), F32),
            pltpu.VMEM((SUBLANES, conv_dim), F32),
            pltpu.VMEM((rows, SSD_INNER), F32),
            pltpu.VMEM((rows, SSD_GROUPS * SSD_STATE), F32),
            pltpu.VMEM((rows, SSD_GROUPS * SSD_STATE), F32),
            pltpu.VMEM((rows, SSD_INNER), F32),
            pltpu.VMEM((rows, SSD_INNER), BF16),
            pltpu.VMEM((rows, LANES), F32),
            pltpu.VMEM((n_ch, SSD_HEADS, q), F32),
            pltpu.VMEM((SSD_GROUPS, SSD_GROUP_WIDTH, SSD_STATE), F32),
        ],
        compiler_params=_params("arbitrary", "arbitrary"),
        name="ssd_mixer",
    )(*args)


S5_GB = LANES // S5_GROUP_WIDTH


def _s5_pack_kernel(x_ref, g_ref, o_ref, h_scr):
    h = _rms(x_ref[...], g_ref[...])
    rows = o_ref.shape[1]
    for gb in range(o_ref.shape[0]):
        h_scr[gb] = h[:, gb * LANES:(gb + 1) * LANES]
    for s in range(S5_SUB):
        for gb in range(o_ref.shape[0]):
            piece = h_scr[gb, pl.ds(s, rows, stride=S5_SUB), :]
            o_ref[gb, :, s * LANES:(s + 1) * LANES] = piece.astype(o_ref.dtype)


def _s5_scan_kernel(u_ref, mop_ref, gop_ref, cop_ref, lam_ref, y_ref, g_scr, hp_scr, *, n_bt, n_sub):
    n_half = g_scr.shape[0] // 2
    tile = 2 * LANES
    u = u_ref[...]
    g_in = jnp.dot(u, gop_ref[...], preferred_element_type=F32)
    for c in range(2 * n_half):
        g_scr[c] = g_in[:, c * LANES:(c + 1) * LANES]
    lam = lam_ref[...]
    lr = [jnp.broadcast_to(lam[0:1, c * LANES:(c + 1) * LANES], (n_bt, LANES)) for c in range(n_half)]
    li = [jnp.broadcast_to(lam[1:2, c * LANES:(c + 1) * LANES], (n_bt, LANES)) for c in range(n_half)]
    hr = [jnp.zeros((n_bt, LANES), F32)] * n_half
    hi = [jnp.zeros((n_bt, LANES), F32)] * n_half
    for j in range(n_sub):
        rows = pl.ds(j, n_bt, stride=n_sub)
        for c in range(n_half):
            hp_scr[c, rows, :] = hr[c]
            hp_scr[n_half + c, rows, :] = hi[c]
            gr, gi = g_scr[c, rows, :], g_scr[n_half + c, rows, :]
            hr[c], hi[c] = lr[c] * hr[c] - li[c] * hi[c] + gr, lr[c] * hi[c] + li[c] * hr[c] + gi
    h_in = jnp.concatenate([hp_scr[c] for c in range(2 * n_half)], axis=1).astype(BF16)
    y_inter = jnp.dot(h_in, cop_ref[...], preferred_element_type=F32)
    n_t = u.shape[1] // tile
    for tp in range(n_t):
        k_hi = (tp + 1) * tile
        y_intra = jnp.dot(u[:, :k_hi], mop_ref[(n_t - 1 - tp) * tile:, :], preferred_element_type=F32)
        y_ref[:, tp * tile:(tp + 1) * tile] = y_inter[:, tp * tile:(tp + 1) * tile] + y_intra


def _s5_operators(b_re, b_im, c_re, c_im, lam_re, lam_im, log_step):
    hp = lax.Precision.HIGHEST
    s, gb, w = S5_SUB, S5_GB, S5_GROUP_WIDTH
    n_g, n_p = lam_re.shape
    n_blk = n_g // gb
    lam = lax.complex(lam_re.astype(F32), lam_im.astype(F32))
    step = jnp.exp(log_step.astype(F32))[:, None]
    tau = jnp.arange(s + 1, dtype=F32)[:, None, None]
    lam_pow = jnp.exp((lam * step)[None] * tau)
    b_bar = ((lam_pow[1] - 1.0) / lam)[..., None] * lax.complex(b_re.astype(F32), b_im.astype(F32))
    cc = lax.complex(c_re.astype(F32), c_im.astype(F32))
    eye = jnp.eye(gb, dtype=F32)
    kern = jnp.einsum('gcp,tgp,gpd->tgcd', cc, lam_pow[:s], b_bar, precision=hp).real
    bd = jnp.einsum('tagcd,gh->atgdhc', kern.reshape(s, n_blk, gb, w, w), eye).reshape(n_blk, s, gb * w, gb * w)
    bd = jnp.concatenate([bd, jnp.zeros_like(bd[:, :1])], axis=1)
    two = jnp.arange(2)
    tile_lag = jnp.arange(s // 2 - 1, -1, -1)
    lag = 2 * tile_lag[:, None, None] + two[None, None, :] - two[None, :, None]
    mop = bd[:, jnp.where(lag < 0, s, lag)]
    mop = mop.transpose(0, 1, 2, 4, 3, 5).reshape(n_blk, s * gb * w, 2 * gb * w)
    ti = jnp.arange(s)
    gm = (lam_pow[s - 1 - ti][:, :, :, None] * b_bar[None]).reshape(s, n_blk, gb, n_p, w)
    gop = jnp.stack([jnp.einsum('sagpd,gh->asgdhp', part, eye) for part in (gm.real, gm.imag)], axis=4)
    gop = gop.reshape(n_blk, s * gb * w, 2 * gb * n_p)
    em = (cc[None] * lam_pow[1:s + 1][:, :, None, :]).reshape(s, n_blk, gb, w, n_p)
    cop = jnp.stack([jnp.einsum('tagcp,gh->agpthc', part, eye) for part in (em.real, -em.imag)], axis=1)
    cop = cop.reshape(n_blk, 2 * gb * n_p, s * gb * w)
    lam_s = lam_pow[s].reshape(n_blk, gb * n_p)
    lam_blk = jnp.stack([lam_s.real, lam_s.imag], axis=1)
    return mop.astype(BF16), gop.astype(BF16), cop.astype(BF16), lam_blk


def _glu_kernel(x_ref, y_ref, g_ref, d_ref, w_ref, b_ref, o_ref, y_scr):
    x = x_ref[...]
    d_model = x.shape[1]
    rows = y_ref.shape[1]
    for s in range(S5_SUB):
        for gb in range(y_ref.shape[0]):
            y_scr[gb, pl.ds(s, rows, stride=S5_SUB), :] = y_ref[gb, :, s * LANES:(s + 1) * LANES]
    y = jnp.concatenate([y_scr[gb] for gb in range(y_ref.shape[0])], axis=1)
    h = _rms(x, g_ref[...])
    act = jax.nn.gelu(y + d_ref[...] * h).astype(BF16)
    vg = jnp.dot(act, w_ref[...], preferred_element_type=F32) + b_ref[...]
    o_ref[...] = x + vg[:, :d_model] * jax.nn.sigmoid(vg[:, d_model:])


def _s5_call(x, bsz, seq, g_mix, b_re, b_im, c_re, c_im, lam_re, lam_im, log_step, d, glu_w, glu_b):
    t, dm = x.shape
    s = S5_SUB
    n_blk = dm // LANES
    n_sub = seq // s
    rows = t // s
    width = s * LANES
    state_w = 2 * S5_GB * S5_STATE
    n_bt = 4
    tm = 512
    assert seq % s == 0 and bsz % n_bt == 0 and t % tm == 0
    mop, gop, cop, lam_blk = _s5_operators(b_re, b_im, c_re, c_im, lam_re, lam_im, log_step)
    tok = lambda i: (i, 0)
    packed = lambda i: (0, i, 0)
    u = pl.pallas_call(
        _s5_pack_kernel,
        grid=(t // tm,),
        in_specs=[pl.BlockSpec((tm, dm), tok), _resident((1, dm))],
        out_specs=pl.BlockSpec((n_blk, tm // s, width), packed),
        out_shape=jax.ShapeDtypeStruct((n_blk, rows, width), BF16),
        scratch_shapes=[pltpu.VMEM((n_blk, tm, LANES), F32)],
        compiler_params=_params("parallel"),
        name="s5_pack",
    )(x, g_mix.reshape(1, dm))
    rt = n_bt * n_sub
    per_blk = lambda a, r: (a, 0, 0)
    once = pl.Buffered(1)
    y = pl.pallas_call(
        functools.partial(_s5_scan_kernel, n_bt=n_bt, n_sub=n_sub),
        grid=(n_blk, rows // rt),
        in_specs=[
            pl.BlockSpec((None, rt, width), lambda a, r: (a, r, 0)),
            pl.BlockSpec((None, width, 2 * LANES), per_blk, pipeline_mode=once),
            pl.BlockSpec((None, width, state_w), per_blk, pipeline_mode=once),
            pl.BlockSpec((None, state_w, width), per_blk, pipeline_mode=once),
            pl.BlockSpec((None, 2, state_w // 2), per_blk),
        ],
        out_specs=pl.BlockSpec((None, rt, width), lambda a, r: (a, r, 0)),
        out_shape=jax.ShapeDtypeStruct((n_blk, rows, width), F32),
        scratch_shapes=[pltpu.VMEM((state_w // LANES, rt, LANES), F32)] * 2,
        compiler_params=_params("arbitrary", "arbitrary"),
        name="s5_scan",
    )(u, mop, gop, cop, lam_blk)
    return pl.pallas_call(
        _glu_kernel,
        grid=(t // tm,),
        in_specs=[
            pl.BlockSpec((tm, dm), tok),
            pl.BlockSpec((n_blk, tm // s, width), packed),
            _resident((1, dm)), _resident((1, dm)), _resident((dm, 2 * dm)), _resident((1, 2 * dm)),
        ],
        out_specs=pl.BlockSpec((tm, dm), tok),
        out_shape=jax.ShapeDtypeStruct((t, dm), F32),
        scratch_shapes=[pltpu.VMEM((n_blk, tm, LANES), F32)],
        compiler_params=_params("parallel"),
        name="s5_glu",
    )(x, y, g_mix.reshape(1, dm), d.reshape(1, dm), glu_w.astype(BF16), glu_b.reshape(1, 2 * dm))


def _rope128(tile, cos, sin_hi, sin_lo):
    return tile * cos + pltpu.roll(tile, 32, 1) * sin_hi + pltpu.roll(tile, LANES - 32, 1) * sin_lo


def _dsa_proj_kernel(x_ref, g_ref, wq_ref, wqi_ref, wsm_ref, tab_ref, tabh_ref,
                     q_ref, qi_ref, k_ref, v_ref, ki_ref, wi_ref):
    h = _rms(x_ref[...], g_ref[...]).astype(BF16)
    cos, s_hi, s_lo = tab_ref[0], tab_ref[1], tab_ref[2]
    q = jnp.dot(h, wq_ref[...], preferred_element_type=F32)
    scale = ATTN_HEAD_DIM ** -0.5
    for j in range(q.shape[1] // LANES):
        sl = slice(j * LANES, (j + 1) * LANES)
        q_ref[:, sl] = (_rope128(q[:, sl], cos, s_hi, s_lo) * scale).astype(BF16)
    qi = jnp.dot(h, wqi_ref[...], preferred_element_type=F32)
    for j in range(qi.shape[1] // LANES):
        sl = slice(j * LANES, (j + 1) * LANES)
        qi_ref[:, sl] = _rope128(qi[:, sl], cos, s_hi, s_lo).astype(BF16)
    sm = jnp.dot(h, wsm_ref[...], preferred_element_type=F32)
    kv = _rope128(sm[:, :LANES], tabh_ref[0], tabh_ref[1], tabh_ref[2])
    kw = _rope128(sm[:, LANES:], tabh_ref[3], tabh_ref[1], tabh_ref[2])
    k_ref[...] = kv[:, :ATTN_HEAD_DIM].astype(BF16)
    v_ref[...] = kv[:, ATTN_HEAD_DIM:].astype(BF16)
    ki_ref[...] = kw[:, :IDX_DIM].astype(BF16)
    wi_ref[...] = kw


def _reduce_rows(x, op):
    slab = 8 * SUBLANES
    part = op(x.reshape(x.shape[0] // slab, slab, x.shape[1]), axis=0)
    return op(part, axis=0, keepdims=True)


def _dsa_attn_kernel(q_ref, qi_ref, wi_ref, k_ref, vt_ref, ki_ref, o_ref,
                     key_scr, bias_scr, qs_scr, qis_scr, og_scr, lga_scr, lgb_scr, *, top_k, span_ends):
    n = pl.program_id(1)
    qb = key_scr.shape[1]
    hg, dh = DSA_HEAD_GROUP, ATTN_HEAD_DIM
    for g in range(qs_scr.shape[0]):
        qs_scr[g] = jnp.concatenate([q_ref[:, (g * hg + j) * dh:(g * hg + j + 1) * dh] for j in range(hg)], axis=0)
    for g in range(qis_scr.shape[0]):
        qis_scr[g] = jnp.concatenate(
            [qi_ref[:, (g * hg + j) * IDX_DIM:(g * hg + j + 1) * IDX_DIM] for j in range(hg)], axis=0)
    for lo, hi in zip((0,) + span_ends[:-1], span_ends):
        @pl.when((n >= lo) & (n < hi))
        def _(hi=hi):
            _dsa_attn_block(wi_ref, k_ref, vt_ref, ki_ref, key_scr, bias_scr, qs_scr, qis_scr, og_scr,
                            lga_scr, lgb_scr,
                            top_k=top_k, seq=hi * qb)
    for g in range(og_scr.shape[0]):
        o_ref[:, g * hg * dh:(g + 1) * hg * dh] = og_scr[g]


def _dsa_attn_block(wi_ref, k_ref, vt_ref, ki_ref, key_full, bias_full, qs_scr, qis_scr, og_scr,
                    lga_scr, lgb_scr, *, top_k, seq):
    n = pl.program_id(1)
    qb = key_full.shape[1]
    key_scr = key_full.at[0:seq, :]
    bias_scr = bias_full.at[0:seq, :]
    kib = ki_ref[0:seq, :]
    wi = wi_ref[...]
    hgi = DSA_HEAD_GROUP
    idx = jnp.zeros((seq, qb), F32)
    for g in range(qis_scr.shape[0]):
        sc = lax.dot_general(kib, qis_scr[g], _NT, preferred_element_type=F32)
        for j in range(hgi):
            hh = g * hgi + j
            idx = idx + jnp.maximum(sc[:, j * qb:(j + 1) * qb], 0.0) * wi[hh:hh + 1, :]
    t_pos = n * qb + lax.broadcasted_iota(jnp.int32, (1, qb), 1)
    s_pos = lax.broadcasted_iota(jnp.int32, (seq, 1), 0)
    causal = s_pos <= t_pos
    idx = jnp.where(causal, idx, -jnp.inf)
    bits = lax.bitcast_convert_type(idx, jnp.int32)
    key_scr[...] = jnp.where(bits < 0, bits ^ jnp.int32(0x7FFFFFFF), bits)

    def count(pred):
        return _reduce_rows(jnp.where(pred, 1.0, 0.0), jnp.sum)

    int_min = jnp.int32(-2 ** 31)
    kf = float(top_k)
    thr0 = jnp.where(count(key_scr[...] >= 0) >= kf, jnp.int32(0), int_min)

    def thr_step(i, thr):
        cand = thr + lax.shift_left(jnp.int32(1), jnp.int32(30) - i)
        return jnp.where(count(key_scr[...] >= cand) >= kf, cand, thr)

    thr = lax.fori_loop(0, 31, thr_step, thr0)
    keys = key_scr[...]
    n_ge = count(keys >= thr)

    def tie_cut():
        need = kf - count(key_scr[...] > thr)
        n_bits = (seq - 1).bit_length()

        def cut_step(i, cut):
            cand = cut + lax.shift_left(jnp.int32(1), jnp.int32(n_bits - 1) - i)
            return jnp.where(count((key_scr[...] == thr) & (s_pos < cand)) < need, cand, cut)

        return lax.fori_loop(0, n_bits, cut_step, jnp.zeros((1, qb), jnp.int32))

    cut = lax.cond(jnp.max(n_ge) > kf, tie_cut, lambda: jnp.full((1, qb), seq, jnp.int32))
    sel = (keys > thr) | ((keys == thr) & (s_pos <= cut))
    bias_scr[...] = jnp.where(sel & causal, 0.0, -jnp.inf)

    hg = DSA_HEAD_GROUP

    n_grp = og_scr.shape[0]
    lg_even = lga_scr.at[0:seq, :]
    lg_odd = lgb_scr.at[0:seq, :]

    def scores(g):
        return lax.dot_general(k_ref[0:seq, :], qs_scr[g], _NT, preferred_element_type=F32)

    def finish(g, lg):
        lg = lg + jnp.concatenate([bias_scr[...]] * hg, axis=1)
        p = jnp.exp((lg - _reduce_rows(lg, jnp.max)).astype(BF16))
        ov = jnp.dot(vt_ref[:, 0:seq], p, preferred_element_type=F32)
        og = ov[0:ATTN_HEAD_DIM, :] / ov[ATTN_HEAD_DIM:ATTN_HEAD_DIM + 1, :]
        pairs = []
        for j in range(0, hg, 2):
            pair = jnp.concatenate([og[:, j * qb:(j + 1) * qb], og[:, (j + 1) * qb:(j + 2) * qb]], axis=0)
            pairs.append(pair.T)
        og_scr[g] = jnp.concatenate(pairs, axis=1).astype(BF16)

    lg_even[...] = scores(0)

    def group_pair(i, carry):
        g = 2 * i
        lg_odd[...] = scores(g + 1)
        finish(g, lg_even[...])
        lg_even[...] = scores(jnp.minimum(g + 2, n_grp - 1))
        finish(g + 1, lg_odd[...])
        return carry

    lax.fori_loop(0, n_grp // 2, group_pair, 0)


def _proj_res_kernel(a_ref, w_ref, x_ref, o_ref):
    o_ref[...] = x_ref[...] + jnp.dot(a_ref[...], w_ref[...], preferred_element_type=F32)


def _rope_tables(seq):
    half = ATTN_HEAD_DIM // 2
    inv = ROPE_THETA ** (-jnp.arange(half, dtype=F32) / half)
    ang = jnp.arange(seq, dtype=F32)[:, None] * inv
    cos, sin = jnp.cos(ang), jnp.sin(ang)
    zero, one = jnp.zeros_like(cos), jnp.ones_like(cos)
    cos64 = jnp.concatenate([cos, cos], axis=1)
    hi64 = jnp.concatenate([zero, sin], axis=1)
    lo64 = jnp.concatenate([-sin, zero], axis=1)
    full = jnp.stack([jnp.tile(t, (1, 2)) for t in (cos64, hi64, lo64)])
    ident = jnp.concatenate([one, one], axis=1)
    zero64 = jnp.concatenate([zero, zero], axis=1)
    wscale = ident * (IDX_HEADS ** -0.5 * IDX_DIM ** -0.5)
    halfs = jnp.stack([jnp.concatenate([cos64, ident], axis=1), jnp.concatenate([hi64, zero64], axis=1),
                       jnp.concatenate([lo64, zero64], axis=1), jnp.concatenate([cos64, wscale], axis=1)])
    return full, halfs


def _dsa_call(x, bsz, seq, g_mix, w_in, w_out):
    t, dm = x.shape
    dq = ATTN_HEADS * ATTN_HEAD_DIM
    dqi = IDX_HEADS * IDX_DIM
    o_k, o_v, o_qi = dq, dq + ATTN_HEAD_DIM, dq + 2 * ATTN_HEAD_DIM
    o_ki = o_qi + dqi
    o_wi = o_ki + IDX_DIM
    wq = w_in[:, :o_k].astype(BF16)
    wqi = w_in[:, o_qi:o_ki].astype(BF16)
    wsm = jnp.concatenate([w_in[:, o_k:o_qi], w_in[:, o_ki:],
                           jnp.zeros((dm, LANES - IDX_DIM - IDX_HEADS), w_in.dtype)], axis=1).astype(BF16)
    full, halfs = _rope_tables(seq)
    tm = 256
    nt = seq // tm
    row = lambda i: (i, 0)
    tab = lambda i: (0, i % nt, 0)
    q, qi, k, v, ki, wi = pl.pallas_call(
        _dsa_proj_kernel,
        grid=(t // tm,),
        in_specs=[
            pl.BlockSpec((tm, dm), row), _resident((1, dm)), _resident(wq.shape), _resident(wqi.shape),
            _resident(wsm.shape),
            pl.BlockSpec((3, tm, LANES), tab), pl.BlockSpec((4, tm, LANES), tab),
        ],
        out_specs=[
            pl.BlockSpec((tm, dq), row), pl.BlockSpec((tm, dqi), row),
            pl.BlockSpec((tm, ATTN_HEAD_DIM), row), pl.BlockSpec((tm, ATTN_HEAD_DIM), row),
            pl.BlockSpec((tm, IDX_DIM), row), pl.BlockSpec((tm, LANES), row),
        ],
        out_shape=[
            jax.ShapeDtypeStruct((t, dq), BF16), jax.ShapeDtypeStruct((t, dqi), BF16),
            jax.ShapeDtypeStruct((t, ATTN_HEAD_DIM), BF16), jax.ShapeDtypeStruct((t, ATTN_HEAD_DIM), BF16),
            jax.ShapeDtypeStruct((t, IDX_DIM), BF16), jax.ShapeDtypeStruct((t, LANES), F32),
        ],
        compiler_params=_params("parallel"),
        name="dsa_proj",
    )(x, g_mix.reshape(1, dm), wq, wqi, wsm, full, halfs)

    qb = QUERY_BLOCK
    nb = seq // qb
    top_k = min(TOPK_MAX, seq // 4)
    blk = lambda b, n: (b * nb + n, 0)
    per_b = lambda b, n: (b, 0, 0)
    span_ends = tuple(sorted({e for e in (nb // 8, nb // 4, 3 * nb // 8, nb // 2, 3 * nb // 4, nb) if e > 0}))
    wi_t = wi[:, IDX_DIM:IDX_DIM + IDX_HEADS].reshape(bsz, seq, IDX_HEADS).transpose(0, 2, 1)
    v_t = v.reshape(bsz, seq, ATTN_HEAD_DIM).transpose(0, 2, 1)
    ones_rows = jnp.zeros((bsz, 2 * SUBLANES, seq), BF16).at[:, 0, :].set(1.0)
    v_t = jnp.concatenate([v_t, ones_rows], axis=1)
    o = pl.pallas_call(
        functools.partial(_dsa_attn_kernel, top_k=top_k, span_ends=span_ends),
        grid=(bsz, nb),
        in_specs=[
            pl.BlockSpec((qb, dq), blk), pl.BlockSpec((qb, dqi), blk),
            pl.BlockSpec((None, IDX_HEADS, qb), lambda b, n: (b, 0, n)),
            pl.BlockSpec((None, seq, ATTN_HEAD_DIM), per_b),
            pl.BlockSpec((None, ATTN_HEAD_DIM + 2 * SUBLANES, seq), per_b),
            pl.BlockSpec((None, seq, IDX_DIM), per_b),
        ],
        out_specs=pl.BlockSpec((qb, dq), blk),
        out_shape=jax.ShapeDtypeStruct((t, dq), BF16),
        scratch_shapes=[
            pltpu.VMEM((seq, qb), jnp.int32), pltpu.VMEM((seq, qb), F32),
            pltpu.VMEM((ATTN_HEADS // DSA_HEAD_GROUP, DSA_HEAD_GROUP * qb, ATTN_HEAD_DIM), BF16),
            pltpu.VMEM((IDX_HEADS // DSA_HEAD_GROUP, DSA_HEAD_GROUP * qb, IDX_DIM), BF16),
            pltpu.VMEM((ATTN_HEADS // DSA_HEAD_GROUP, qb, DSA_HEAD_GROUP * ATTN_HEAD_DIM), BF16),
            pltpu.VMEM((seq, DSA_HEAD_GROUP * qb), F32), pltpu.VMEM((seq, DSA_HEAD_GROUP * qb), F32),
        ],
        compiler_params=_params("parallel", "arbitrary"),
        name="dsa_attn",
    )(q, qi, wi_t, k.reshape(bsz, seq, -1), v_t, ki.reshape(bsz, seq, -1))

    tmo = 512
    return pl.pallas_call(
        _proj_res_kernel,
        grid=(t // tmo,),
        in_specs=[pl.BlockSpec((tmo, dq), row), _resident((dq, dm)), pl.BlockSpec((tmo, dm), row)],
        out_specs=pl.BlockSpec((tmo, dm), row),
        out_shape=jax.ShapeDtypeStruct((t, dm), F32),
        compiler_params=_params("parallel"),
        name="dsa_out",
    )(o, w_out.astype(BF16), x)


def kernel(x, ffn1_norm, ffn1_w_in, ffn1_w_out, mix_norm, ffn2_norm, ffn2_w_in, ffn2_w_out, ssd_in_proj, ssd_conv_w, ssd_conv_b, ssd_dt_bias, ssd_a_log, ssd_d, ssd_gate_norm, ssd_out_proj, s5_b_re, s5_b_im, s5_c_re, s5_c_im, s5_lam_re, s5_lam_im, s5_log_step, s5_d, s5_glu_w, s5_glu_b, dsa_in_proj, dsa_out_proj, final_norm):
    bsz, seq, dm = x.shape
    depth = ffn1_norm.shape[0]
    xt = x.reshape(bsz * seq, dm)
    for i in range(depth):
        j, kind = divmod(i, N_MIXERS)
        xt = _ffn_call(xt, ffn1_norm[i], ffn1_w_in[i].astype(BF16), ffn1_w_out[i].astype(BF16))
        if kind == 0:
            xt = _ssd_call(xt, bsz, seq, mix_norm[i], ssd_in_proj[j], ssd_conv_w[j], ssd_conv_b[j],
                           ssd_dt_bias[j], ssd_a_log[j], ssd_d[j], ssd_gate_norm[j], ssd_out_proj[j])
        elif kind == 1:
            xt = _s5_call(xt, bsz, seq, mix_norm[i], s5_b_re[j], s5_b_im[j], s5_c_re[j], s5_c_im[j],
                          s5_lam_re[j], s5_lam_im[j], s5_log_step[j], s5_d[j], s5_glu_w[j], s5_glu_b[j])
        else:
            xt = _dsa_call(xt, bsz, seq, mix_norm[i], dsa_in_proj[j], dsa_out_proj[j])
        post = final_norm if i == depth - 1 else None
        xt = _ffn_call(xt, ffn2_norm[i], ffn2_w_in[i].astype(BF16), ffn2_w_out[i].astype(BF16), post)
    return xt.reshape(bsz, seq, dm)
```

```python
import functools
import math

import jax
import jax.numpy as jnp
from jax import lax
from jax.experimental import pallas as pl
from jax.experimental.pallas import tpu as pltpu

F32 = jnp.float32
BF16 = jnp.bfloat16

NORM_EPS = 1e-6
N_MIXERS = 3

SSD_HEAD_DIM = 64
SSD_GROUPS = 8
SSD_HEADS_PER_GROUP = 4
SSD_HEADS = SSD_GROUPS * SSD_HEADS_PER_GROUP
SSD_STATE = 128
SSD_INNER = SSD_HEADS * SSD_HEAD_DIM
SSD_CONV = 4
SSD_CHUNK_MAX = 256
SSD_GROUP_WIDTH = SSD_HEADS_PER_GROUP * SSD_HEAD_DIM

S5_GROUP_WIDTH = 16
S5_STATE = 64
S5_SUB = 16

ATTN_HEADS = 16
ATTN_HEAD_DIM = 64
IDX_HEADS = 8
IDX_DIM = 64
TOPK_MAX = 256
QUERY_BLOCK = 128
ROPE_THETA = 10000.0
DSA_HEAD_GROUP = 2

LANES = 128
SUBLANES = 8
VMEM_LIMIT_BYTES = 56 * 1024 * 1024

_NT = (((1,), (1,)), ((), ()))


def _params(*semantics):
    return pltpu.CompilerParams(dimension_semantics=semantics, vmem_limit_bytes=VMEM_LIMIT_BYTES)


def _rms(x, g):
    ms = jnp.mean(x * x, axis=-1, keepdims=True)
    return x * lax.rsqrt(ms + NORM_EPS) * g


def _silu(x):
    return x * jax.nn.sigmoid(x)


def _softplus(x):
    return jnp.maximum(x, 0.0) + jnp.log1p(jnp.exp(-jnp.abs(x)))


def _resident(shape):
    zeros = (0,) * len(shape)
    return pl.BlockSpec(shape, lambda *_: zeros, pipeline_mode=pl.Buffered(1))


FFN_CHUNK = 256


def _ffn_kernel(x_ref, g_ref, win_ref, wout_ref, pg_ref, o_ref, *, post_norm):
    x = x_ref[...]
    h = _rms(x, g_ref[...]).astype(BF16)
    f = wout_ref.shape[0]
    acc = jnp.zeros(x.shape, F32)
    for j in range(f // FFN_CHUNK):
        lo = j * FFN_CHUNK
        gate = jnp.dot(h, win_ref[:, lo:lo + FFN_CHUNK], preferred_element_type=F32)
        up = jnp.dot(h, win_ref[:, f + lo:f + lo + FFN_CHUNK], preferred_element_type=F32)
        act = (_silu(gate) * up).astype(BF16)
        acc = acc + jnp.dot(act, wout_ref[lo:lo + FFN_CHUNK, :], preferred_element_type=F32)
    y = x + 0.5 * acc
    if post_norm:
        y = _rms(y, pg_ref[...])
    o_ref[...] = y


def _ffn_call(x, g, w_in, w_out, post_g=None):
    t, d = x.shape
    f = w_out.shape[0]
    tm = 1024
    assert t % tm == 0 and f % FFN_CHUNK == 0
    post_norm = post_g is not None
    pg = post_g if post_norm else g
    return pl.pallas_call(
        functools.partial(_ffn_kernel, post_norm=post_norm),
        grid=(t // tm,),
        in_specs=[
            pl.BlockSpec((tm, d), lambda i: (i, 0)),
            _resident((1, d)), _resident(w_in.shape), _resident(w_out.shape), _resident((1, d)),
        ],
        out_specs=pl.BlockSpec((tm, d), lambda i: (i, 0)),
        out_shape=jax.ShapeDtypeStruct((t, d), F32),
        compiler_params=_params("parallel"),
        name="ffn",
    )(x, g.reshape(1, d), w_in, w_out, pg.reshape(1, d))


def _split3(a):
    hi = a.astype(BF16)
    r1 = a - hi.astype(F32)
    mid = r1.astype(BF16)
    lo = (r1 - mid.astype(F32)).astype(BF16)
    return hi, mid, lo


def _ssd_kernel(x_ref, gmix_ref, wz_ref, wxbc_ref, wdt_ref, wdtt_ref, convw_ref, convb_ref,
                dtbc_ref, dtbr_ref, ac_ref, ar_ref, dvec_ref, gn_ref, wout_ref, selg_ref, selh_ref,
                o_ref,
                raw_scr, carry_scr, xs_scr, bm_scr, cm_scr, z_scr, yn_scr, dtc_scr, dtr_scr, state_scr,
                *, q, n_ch):
    pad = SUBLANES
    rows = n_ch * q
    gw = SSD_GROUP_WIDTH
    hd_dim = SSD_HEAD_DIM
    n_state = SSD_STATE
    hpg = SSD_HEADS_PER_GROUP

    @pl.when(pl.program_id(1) == 0)
    def _():
        state_scr[...] = jnp.zeros_like(state_scr)
        carry_scr[...] = jnp.zeros_like(carry_scr)

    x = x_ref[...]
    h = _rms(x, gmix_ref[...]).astype(BF16)
    cw = raw_scr.shape[1]
    n_x = SSD_INNER // cw
    n_b = SSD_GROUPS * n_state // cw
    n_cols = wxbc_ref.shape[1] // cw
    zw = wz_ref.shape[1] // n_cols
    for j in range(n_cols):
        sl = slice(j * cw, (j + 1) * cw)
        raw = raw_scr
        raw[0:pad, :] = carry_scr[:, sl]
        raw[pad:pad + rows, :] = jnp.dot(h, wxbc_ref[:, sl], preferred_element_type=F32)
        carry_scr[:, sl] = raw[rows:rows + pad, :]
        z_scr[:, j * zw:(j + 1) * zw] = jnp.dot(h, wz_ref[:, j * zw:(j + 1) * zw], preferred_element_type=F32)
        w = convw_ref[:, sl]
        acc = raw[pad - 3:pad - 3 + rows, :] * w[0:1]
        acc = acc + raw[pad - 2:pad - 2 + rows, :] * w[1:2]
        acc = acc + raw[pad - 1:pad - 1 + rows, :] * w[2:3]
        acc = acc + raw[pad:pad + rows, :] * w[3:4]
        act = _silu(acc + convb_ref[:, sl])
        if j < n_x:
            xs_scr[:, sl] = act
        elif j < n_x + n_b:
            bm_scr[:, (j - n_x) * cw:(j - n_x + 1) * cw] = act
        else:
            cm_scr[:, (j - n_x - n_b) * cw:(j - n_x - n_b + 1) * cw] = act

    dtc_scr[...] = _softplus(jnp.dot(h, wdt_ref[...], preferred_element_type=F32) + dtbc_ref[...])
    dt_rows = _softplus(lax.dot_general(wdtt_ref[...], h, _NT, preferred_element_type=F32) + dtbr_ref[...])
    for ci in range(n_ch):
        dtr_scr[ci] = dt_rows[:, ci * q:(ci + 1) * q]
    row = lax.broadcasted_iota(jnp.int32, (q, q), 0)
    col = lax.broadcasted_iota(jnp.int32, (q, q), 1)
    causal = col <= row
    tri_low = jnp.where(causal, 1.0, 0.0).astype(BF16)
    tri_up = jnp.where(row <= col, 1.0, 0.0).astype(BF16)

    def expand(a, sel):
        return sum(jnp.dot(p, sel, preferred_element_type=F32) for p in _split3(a))

    def chunk(ci, carry):
        r = pl.ds(pl.multiple_of(ci * q, q), q)
        dt_r = dtr_scr[ci]
        a_c = dtc_scr[r, :] * ac_ref[...]
        a_r = dt_r * ar_ref[...]
        cs_c = sum(jnp.dot(tri_low, p, preferred_element_type=F32) for p in _split3(a_c))
        cs_r = sum(jnp.dot(p, tri_up, preferred_element_type=F32) for p in _split3(a_r))
        e_c = jnp.exp(cs_c)
        cs_end = cs_r[:, q - 1:q]
        f_r = dt_r * jnp.exp(cs_end - cs_r)
        e_end = jnp.exp(cs_end)
        for g in range(SSD_GROUPS):
            bg = bm_scr[r, g * n_state:(g + 1) * n_state].astype(BF16)
            cg = cm_scr[r, g * n_state:(g + 1) * n_state].astype(BF16)
            cb = lax.dot_general(cg, bg, _NT, preferred_element_type=F32)
            xs_g = xs_scr[r, g * gw:(g + 1) * gw]
            xs_gb = xs_g.astype(BF16)
            xs_gt = xs_g.T
            st_g = state_scr[g]
            y_off = lax.dot_general(cg, st_g.astype(BF16), _NT, preferred_element_type=F32)
            cs_cols = expand(cs_c, selh_ref[g])
            ys, ws, decs = [], [], []
            for k in range(hpg):
                hd = g * hpg + k
                cs_col = cs_cols[:, k * LANES:(k + 1) * LANES]
                seg = jnp.concatenate([cs_col] * (q // LANES), axis=1) - cs_r[hd:hd + 1, :]
                decay = jnp.exp(jnp.where(causal, seg, -jnp.inf))
                m = (cb * decay * dt_r[hd:hd + 1, :]).astype(BF16)
                ys.append(jnp.dot(m, xs_gb[:, k * hd_dim:(k + 1) * hd_dim], preferred_element_type=F32))
                ws.append(xs_gt[k * hd_dim:(k + 1) * hd_dim, :] * f_r[hd:hd + 1, :])
                decs.append(jnp.broadcast_to(e_end[hd:hd + 1, :], (hd_dim, n_state)))
            wt = jnp.concatenate(ws, axis=0)
            upd = jnp.dot(wt.astype(BF16), bg, preferred_element_type=F32)
            state_scr[g] = st_g * jnp.concatenate(decs, axis=0) + upd
            y_g = jnp.concatenate(ys, axis=1) + y_off * expand(e_c, selg_ref[g])
            y_g = y_g + dvec_ref[:, g * gw:(g + 1) * gw] * xs_g
            yg = y_g * _silu(z_scr[r, g * gw:(g + 1) * gw])
            yg = yg * lax.rsqrt(jnp.mean(yg * yg, axis=-1, keepdims=True) + NORM_EPS)
            yn_scr[r, g * gw:(g + 1) * gw] = (yg * gn_ref[:, g * gw:(g + 1) * gw]).astype(BF16)
        o_ref[r, :] = x_ref[r, :] + jnp.dot(yn_scr[r, :], wout_ref[...], preferred_element_type=F32)
        return carry

    lax.fori_loop(0, n_ch, chunk, 0)


def _ssd_call(x, bsz, seq, g_mix, in_proj, conv_w, conv_b, dt_bias, a_log, d, gate_norm, out_proj):
    t, dm = x.shape
    q = math.gcd(seq, SSD_CHUNK_MAX)
    nc = seq // q
    conv_dim = SSD_INNER + 2 * SSD_GROUPS * SSD_STATE
    wz = in_proj[:, :SSD_INNER].astype(BF16)
    wxbc = in_proj[:, SSD_INNER:SSD_INNER + conv_dim].astype(BF16)
    wdt = in_proj[:, SSD_INNER + conv_dim:]
    wdt_c = jnp.pad(wdt, ((0, 0), (0, LANES - SSD_HEADS))).astype(BF16)
    wdt_r = wdt.T.astype(BF16)
    a = -jnp.exp(a_log.astype(F32))
    lane_pad = (0, LANES - SSD_HEADS)
    head_of = jnp.arange(LANES)
    grp_heads = jnp.arange(SSD_GROUPS) * SSD_HEADS_PER_GROUP
    args = (
        x, g_mix.reshape(1, dm), wz, wxbc, wdt_c, wdt_r, conv_w, conv_b.reshape(1, conv_dim),
        jnp.pad(dt_bias, lane_pad).reshape(1, LANES), dt_bias.reshape(SSD_HEADS, 1),
        jnp.pad(a, lane_pad).reshape(1, LANES), a.reshape(SSD_HEADS, 1),
        jnp.repeat(d, SSD_HEAD_DIM).reshape(1, SSD_INNER), gate_norm.reshape(1, SSD_INNER),
        out_proj.astype(BF16),
        (head_of[:, None] == (grp_heads[:, None, None] + jnp.arange(SSD_GROUP_WIDTH) // SSD_HEAD_DIM)).astype(BF16),
        (head_of[:, None] == (grp_heads[:, None, None] + jnp.arange(SSD_HEADS_PER_GROUP * LANES) // LANES)).astype(BF16),
    )
    n_ch = math.gcd(nc, 2)
    steps = nc // n_ch
    rows = n_ch * q
    x_spec = pl.BlockSpec((rows, dm), lambda b, c: (b * steps + c, 0))
    in_specs = [x_spec] + [_resident(a_.shape) for a_ in args[1:]]
    return pl.pallas_call(
        functools.partial(_ssd_kernel, q=q, n_ch=n_ch),
        grid=(bsz, steps),
        in_specs=in_specs,
        out_specs=x_spec,
        out_shape=jax.ShapeDtypeStruct((t, dm), F32),
        scratch_shapes=[
            pltpu.VMEM((rows + 2 * SUBLANES, 4 * LANES), F32),
            pltpu.VMEM((SUBLANES, conv_dim), F32),
            pltpu.VMEM((rows, SSD_INNER), F32),
            pltpu.VMEM((rows, SSD_GROUPS * SSD_STATE), F32),
            pltpu.VMEM((rows, SSD_GROUPS * SSD_STATE), F32),
            pltpu.VMEM((rows, SSD_INNER), F32),
            pltpu.VMEM((rows, SSD_INNER), BF16),
            pltpu.VMEM((rows, LANES), F32),
            pltpu.VMEM((n_ch, SSD_HEADS, q), F32),
            pltpu.VMEM((SSD_GROUPS, SSD_GROUP_WIDTH, SSD_STATE), F32),
        ],
        compiler_params=_params("arbitrary", "arbitrary"),
        name="ssd_mixer",
    )(*args)


S5_GB = LANES // S5_GROUP_WIDTH


def _s5_pack_kernel(x_ref, g_ref, o_ref, h_scr):
    h = _rms(x_ref[...], g_ref[...])
    rows = o_ref.shape[1]
    for gb in range(o_ref.shape[0]):
        h_scr[gb] = h[:, gb * LANES:(gb + 1) * LANES]
    for s in range(S5_SUB):
        for gb in range(o_ref.shape[0]):
            piece = h_scr[gb, pl.ds(s, rows, stride=S5_SUB), :]
            o_ref[gb, :, s * LANES:(s + 1) * LANES] = piece.astype(o_ref.dtype)


def _s5_scan_kernel(u_ref, mop_ref, gop_ref, cop_ref, lam_ref, y_ref, g_scr, hp_scr, *, n_bt, n_sub):
    n_half = g_scr.shape[0] // 2
    tile = 2 * LANES
    u = u_ref[...]
    g_in = jnp.dot(u, gop_ref[...], preferred_element_type=F32)
    for c in range(2 * n_half):
        g_scr[c] = g_in[:, c * LANES:(c + 1) * LANES]
    lam = lam_ref[...]
    lr = [jnp.broadcast_to(lam[0:1, c * LANES:(c + 1) * LANES], (n_bt, LANES)) for c in range(n_half)]
    li = [jnp.broadcast_to(lam[1:2, c * LANES:(c + 1) * LANES], (n_bt, LANES)) for c in range(n_half)]
    hr = [jnp.zeros((n_bt, LANES), F32)] * n_half
    hi = [jnp.zeros((n_bt, LANES), F32)] * n_half
    for j in range(n_sub):
        rows = pl.ds(j, n_bt, stride=n_sub)
        for c in range(n_half):
            hp_scr[c, rows, :] = hr[c]
            hp_scr[n_half + c, rows, :] = hi[c]
            gr, gi = g_scr[c, rows, :], g_scr[n_half + c, rows, :]
            hr[c], hi[c] = lr[c] * hr[c] - li[c] * hi[c] + gr, lr[c] * hi[c] + li[c] * hr[c] + gi
    h_in = jnp.concatenate([hp_scr[c] for c in range(2 * n_half)], axis=1).astype(BF16)
    y_inter = jnp.dot(h_in, cop_ref[...], preferred_element_type=F32)
    n_t = u.shape[1] // tile
    for tp in range(n_t):
        k_hi = (tp + 1) * tile
        y_intra = jnp.dot(u[:, :k_hi], mop_ref[(n_t - 1 - tp) * tile:, :], preferred_element_type=F32)
        y_ref[:, tp * tile:(tp + 1) * tile] = y_inter[:, tp * tile:(tp + 1) * tile] + y_intra


def _s5_operators(b_re, b_im, c_re, c_im, lam_re, lam_im, log_step):
    hp = lax.Precision.HIGHEST
    s, gb, w = S5_SUB, S5_GB, S5_GROUP_WIDTH
    n_g, n_p = lam_re.shape
    n_blk = n_g // gb
    lam = lax.complex(lam_re.astype(F32), lam_im.astype(F32))
    step = jnp.exp(log_step.astype(F32))[:, None]
    tau = jnp.arange(s + 1, dtype=F32)[:, None, None]
    lam_pow = jnp.exp((lam * step)[None] * tau)
    b_bar = ((lam_pow[1] - 1.0) / lam)[..., None] * lax.complex(b_re.astype(F32), b_im.astype(F32))
    cc = lax.complex(c_re.astype(F32), c_im.astype(F32))
    eye = jnp.eye(gb, dtype=F32)
    kern = jnp.einsum('gcp,tgp,gpd->tgcd', cc, lam_pow[:s], b_bar, precision=hp).real
    bd = jnp.einsum('tagcd,gh->atgdhc', kern.reshape(s, n_blk, gb, w, w), eye).reshape(n_blk, s, gb * w, gb * w)
    bd = jnp.concatenate([bd, jnp.zeros_like(bd[:, :1])], axis=1)
    two = jnp.arange(2)
    tile_lag = jnp.arange(s // 2 - 1, -1, -1)
    lag = 2 * tile_lag[:, None, None] + two[None, None, :] - two[None, :, None]
    mop = bd[:, jnp.where(lag < 0, s, lag)]
    mop = mop.transpose(0, 1, 2, 4, 3, 5).reshape(n_blk, s * gb * w, 2 * gb * w)
    ti = jnp.arange(s)
    gm = (lam_pow[s - 1 - ti][:, :, :, None] * b_bar[None]).reshape(s, n_blk, gb, n_p, w)
    gop = jnp.stack([jnp.einsum('sagpd,gh->asgdhp', part, eye) for part in (gm.real, gm.imag)], axis=4)
    gop = gop.reshape(n_blk, s * gb * w, 2 * gb * n_p)
    em = (cc[None] * lam_pow[1:s + 1][:, :, None, :]).reshape(s, n_blk, gb, w, n_p)
    cop = jnp.stack([jnp.einsum('tagcp,gh->agpthc', part, eye) for part in (em.real, -em.imag)], axis=1)
    cop = cop.reshape(n_blk, 2 * gb * n_p, s * gb * w)
    lam_s = lam_pow[s].reshape(n_blk, gb * n_p)
    lam_blk = jnp.stack([lam_s.real, lam_s.imag], axis=1)
    return mop.astype(BF16), gop.astype(BF16), cop.astype(BF16), lam_blk


def _glu_kernel(x_ref, y_ref, g_ref, d_ref, w_ref, b_ref, o_ref, y_scr):
    x = x_ref[...]
    d_model = x.shape[1]
    rows = y_ref.shape[1]
    for s in range(S5_SUB):
        for gb in range(y_ref.shape[0]):
            y_scr[gb, pl.ds(s, rows, stride=S5_SUB), :] = y_ref[gb, :, s * LANES:(s + 1) * LANES]
    y = jnp.concatenate([y_scr[gb] for gb in range(y_ref.shape[0])], axis=1)
    h = _rms(x, g_ref[...])
    act = jax.nn.gelu(y + d_ref[...] * h).astype(BF16)
    vg = jnp.dot(act, w_ref[...], preferred_element_type=F32) + b_ref[...]
    o_ref[...] = x + vg[:, :d_model] * jax.nn.sigmoid(vg[:, d_model:])


def _s5_call(x, bsz, seq, g_mix, b_re, b_im, c_re, c_im, lam_re, lam_im, log_step, d, glu_w, glu_b):
    t, dm = x.shape
    s = S5_SUB
    n_blk = dm // LANES
    n_sub = seq // s
    rows = t // s
    width = s * LANES
    state_w = 2 * S5_GB * S5_STATE
    n_bt = 4
    tm = 512
    assert seq % s == 0 and bsz % n_bt == 0 and t % tm == 0
    mop, gop, cop, lam_blk = _s5_operators(b_re, b_im, c_re, c_im, lam_re, lam_im, log_step)
    tok = lambda i: (i, 0)
    packed = lambda i: (0, i, 0)
    u = pl.pallas_call(
        _s5_pack_kernel,
        grid=(t // tm,),
        in_specs=[pl.BlockSpec((tm, dm), tok), _resident((1, dm))],
        out_specs=pl.BlockSpec((n_blk, tm // s, width), packed),
        out_shape=jax.ShapeDtypeStruct((n_blk, rows, width), BF16),
        scratch_shapes=[pltpu.VMEM((n_blk, tm, LANES), F32)],
        compiler_params=_params("parallel"),
        name="s5_pack",
    )(x, g_mix.reshape(1, dm))
    rt = n_bt * n_sub
    per_blk = lambda a, r: (a, 0, 0)
    once = pl.Buffered(1)
    y = pl.pallas_call(
        functools.partial(_s5_scan_kernel, n_bt=n_bt, n_sub=n_sub),
        grid=(n_blk, rows // rt),
        in_specs=[
            pl.BlockSpec((None, rt, width), lambda a, r: (a, r, 0)),
            pl.BlockSpec((None, width, 2 * LANES), per_blk, pipeline_mode=once),
            pl.BlockSpec((None, width, state_w), per_blk, pipeline_mode=once),
            pl.BlockSpec((None, state_w, width), per_blk, pipeline_mode=once),
            pl.BlockSpec((None, 2, state_w // 2), per_blk),
        ],
        out_specs=pl.BlockSpec((None, rt, width), lambda a, r: (a, r, 0)),
        out_shape=jax.ShapeDtypeStruct((n_blk, rows, width), F32),
        scratch_shapes=[pltpu.VMEM((state_w // LANES, rt, LANES), F32)] * 2,
        compiler_params=_params("arbitrary", "arbitrary"),
        name="s5_scan",
    )(u, mop, gop, cop, lam_blk)
    return pl.pallas_call(
        _glu_kernel,
        grid=(t // tm,),
        in_specs=[
            pl.BlockSpec((tm, dm), tok),
            pl.BlockSpec((n_blk, tm // s, width), packed),
            _resident((1, dm)), _resident((1, dm)), _resident((dm, 2 * dm)), _resident((1, 2 * dm)),
        ],
        out_specs=pl.BlockSpec((tm, dm), tok),
        out_shape=jax.ShapeDtypeStruct((t, dm), F32),
        scratch_shapes=[pltpu.VMEM((n_blk, tm, LANES), F32)],
        compiler_params=_params("parallel"),
        name="s5_glu",
    )(x, y, g_mix.reshape(1, dm), d.reshape(1, dm), glu_w.astype(BF16), glu_b.reshape(1, 2 * dm))


def _rope128(tile, cos, sin_hi, sin_lo):
    return tile * cos + pltpu.roll(tile, 32, 1) * sin_hi + pltpu.roll(tile, LANES - 32, 1) * sin_lo


def _dsa_proj_kernel(x_ref, g_ref, wq_ref, wqi_ref, wsm_ref, tab_ref, tabh_ref,
                     q_ref, qi_ref, k_ref, v_ref, ki_ref, wi_ref):
    h = _rms(x_ref[...], g_ref[...]).astype(BF16)
    cos, s_hi, s_lo = tab_ref[0], tab_ref[1], tab_ref[2]
    q = jnp.dot(h, wq_ref[...], preferred_element_type=F32)
    scale = ATTN_HEAD_DIM ** -0.5
    for j in range(q.shape[1] // LANES):
        sl = slice(j * LANES, (j + 1) * LANES)
        q_ref[:, sl] = (_rope128(q[:, sl], cos, s_hi, s_lo) * scale).astype(BF16)
    qi = jnp.dot(h, wqi_ref[...], preferred_element_type=F32)
    for j in range(qi.shape[1] // LANES):
        sl = slice(j * LANES, (j + 1) * LANES)
        qi_ref[:, sl] = _rope128(qi[:, sl], cos, s_hi, s_lo).astype(BF16)
    sm = jnp.dot(h, wsm_ref[...], preferred_element_type=F32)
    kv = _rope128(sm[:, :LANES], tabh_ref[0], tabh_ref[1], tabh_ref[2])
    kw = _rope128(sm[:, LANES:], tabh_ref[3], tabh_ref[1], tabh_ref[2])
    k_ref[...] = kv[:, :ATTN_HEAD_DIM].astype(BF16)
    v_ref[...] = kv[:, ATTN_HEAD_DIM:].astype(BF16)
    ki_ref[...] = kw[:, :IDX_DIM].astype(BF16)
    wi_ref[...] = kw


def _reduce_rows(x, op):
    slab = 8 * SUBLANES
    part = op(x.reshape(x.shape[0] // slab, slab, x.shape[1]), axis=0)
    return op(part, axis=0, keepdims=True)


def _dsa_attn_kernel(q_ref, qi_ref, wi_ref, k_ref, vt_ref, ki_ref, o_ref,
                     key_scr, bias_scr, qs_scr, qis_scr, og_scr, lga_scr, lgb_scr, *, top_k, span_ends):
    n = pl.program_id(1)
    qb = key_scr.shape[1]
    hg, dh = DSA_HEAD_GROUP, ATTN_HEAD_DIM
    for g in range(qs_scr.shape[0]):
        qs_scr[g] = jnp.concatenate([q_ref[:, (g * hg + j) * dh:(g * hg + j + 1) * dh] for j in range(hg)], axis=0)
    for g in range(qis_scr.shape[0]):
        qis_scr[g] = jnp.concatenate(
            [qi_ref[:, (g * hg + j) * IDX_DIM:(g * hg + j + 1) * IDX_DIM] for j in range(hg)], axis=0)
    for lo, hi in zip((0,) + span_ends[:-1], span_ends):
        @pl.when((n >= lo) & (n < hi))
        def _(hi=hi):
            _dsa_attn_block(wi_ref, k_ref, vt_ref, ki_ref, key_scr, bias_scr, qs_scr, qis_scr, og_scr,
                            lga_scr, lgb_scr,
                            top_k=top_k, seq=hi * qb)
    for g in range(og_scr.shape[0]):
        o_ref[:, g * hg * dh:(g + 1) * hg * dh] = og_scr[g]


def _dsa_attn_block(wi_ref, k_ref, vt_ref, ki_ref, key_full, bias_full, qs_scr, qis_scr, og_scr,
                    lga_scr, lgb_scr, *, top_k, seq):
    n = pl.program_id(1)
    qb = key_full.shape[1]
    key_scr = key_full.at[0:seq, :]
    bias_scr = bias_full.at[0:seq, :]
    kib = ki_ref[0:seq, :]
    wi = wi_ref[...]
    hgi = DSA_HEAD_GROUP
    idx = jnp.zeros((seq, qb), F32)
    for g in range(qis_scr.shape[0]):
        sc = lax.dot_general(kib, qis_scr[g], _NT, preferred_element_type=F32)
        for j in range(hgi):
            hh = g * hgi + j
            idx = idx + jnp.maximum(sc[:, j * qb:(j + 1) * qb], 0.0) * wi[hh:hh + 1, :]
    t_pos = n * qb + lax.broadcasted_iota(jnp.int32, (1, qb), 1)
    s_pos = lax.broadcasted_iota(jnp.int32, (seq, 1), 0)
    causal = s_pos <= t_pos
    idx = jnp.where(causal, idx, -jnp.inf)
    bits = lax.bitcast_convert_type(idx, jnp.int32)
    key_scr[...] = jnp.where(bits < 0, bits ^ jnp.int32(0x7FFFFFFF), bits)

    def count(pred):
        return _reduce_rows(jnp.where(pred, 1.0, 0.0), jnp.sum)

    int_min = jnp.int32(-2 ** 31)
    kf = float(top_k)
    thr0 = jnp.where(count(key_scr[...] >= 0) >= kf, jnp.int32(0), int_min)

    def thr_step(i, thr):
        cand = thr + lax.shift_left(jnp.int32(1), jnp.int32(30) - i)
        return jnp.where(count(key_scr[...] >= cand) >= kf, cand, thr)

    thr = lax.fori_loop(0, 31, thr_step, thr0)
    keys = key_scr[...]
    n_ge = count(keys >= thr)

    def tie_cut():
        need = kf - count(key_scr[...] > thr)
        n_bits = (seq - 1).bit_length()

        def cut_step(i, cut):
            cand = cut + lax.shift_left(jnp.int32(1), jnp.int32(n_bits - 1) - i)
            return jnp.where(count((key_scr[...] == thr) & (s_pos < cand)) < need, cand, cut)

        return lax.fori_loop(0, n_bits, cut_step, jnp.zeros((1, qb), jnp.int32))

    cut = lax.cond(jnp.max(n_ge) > kf, tie_cut, lambda: jnp.full((1, qb), seq, jnp.int32))
    sel = (keys > thr) | ((keys == thr) & (s_pos <= cut))
    bias_scr[...] = jnp.where(sel & causal, 0.0, -jnp.inf)

    hg = DSA_HEAD_GROUP

    n_grp = og_scr.shape[0]
    lg_even = lga_scr.at[0:seq, :]
    lg_odd = lgb_scr.at[0:seq, :]

    def scores(g):
        return lax.dot_general(k_ref[0:seq, :], qs_scr[g], _NT, preferred_element_type=F32)

    def finish(g, lg):
        lg = lg + jnp.concatenate([bias_scr[...]] * hg, axis=1)
        p = jnp.exp((lg - _reduce_rows(lg, jnp.max)).astype(BF16))
        ov = jnp.dot(vt_ref[:, 0:seq], p, preferred_element_type=F32)
        og = ov[0:ATTN_HEAD_DIM, :] / ov[ATTN_HEAD_DIM:ATTN_HEAD_DIM + 1, :]
        pairs = []
        for j in range(0, hg, 2):
            pair = jnp.concatenate([og[:, j * qb:(j + 1) * qb], og[:, (j + 1) * qb:(j + 2) * qb]], axis=0)
            pairs.append(pair.T)
        og_scr[g] = jnp.concatenate(pairs, axis=1).astype(BF16)

    lg_even[...] = scores(0)

    def group_pair(i, carry):
        g = 2 * i
        lg_odd[...] = scores(g + 1)
        finish(g, lg_even[...])
        lg_even[...] = scores(jnp.minimum(g + 2, n_grp - 1))
        finish(g + 1, lg_odd[...])
        return carry

    lax.fori_loop(0, n_grp // 2, group_pair, 0)


def _proj_res_kernel(a_ref, w_ref, x_ref, o_ref):
    o_ref[...] = x_ref[...] + jnp.dot(a_ref[...], w_ref[...], preferred_element_type=F32)


def _rope_tables(seq):
    half = ATTN_HEAD_DIM // 2
    inv = ROPE_THETA ** (-jnp.arange(half, dtype=F32) / half)
    ang = jnp.arange(seq, dtype=F32)[:, None] * inv
    cos, sin = jnp.cos(ang), jnp.sin(ang)
    zero, one = jnp.zeros_like(cos), jnp.ones_like(cos)
    cos64 = jnp.concatenate([cos, cos], axis=1)
    hi64 = jnp.concatenate([zero, sin], axis=1)
    lo64 = jnp.concatenate([-sin, zero], axis=1)
    full = jnp.stack([jnp.tile(t, (1, 2)) for t in (cos64, hi64, lo64)])
    ident = jnp.concatenate([one, one], axis=1)
    zero64 = jnp.concatenate([zero, zero], axis=1)
    wscale = ident * (IDX_HEADS ** -0.5 * IDX_DIM ** -0.5)
    halfs = jnp.stack([jnp.concatenate([cos64, ident], axis=1), jnp.concatenate([hi64, zero64], axis=1),
                       jnp.concatenate([lo64, zero64], axis=1), jnp.concatenate([cos64, wscale], axis=1)])
    return full, halfs


def _dsa_call(x, bsz, seq, g_mix, w_in, w_out):
    t, dm = x.shape
    dq = ATTN_HEADS * ATTN_HEAD_DIM
    dqi = IDX_HEADS * IDX_DIM
    o_k, o_v, o_qi = dq, dq + ATTN_HEAD_DIM, dq + 2 * ATTN_HEAD_DIM
    o_ki = o_qi + dqi
    o_wi = o_ki + IDX_DIM
    wq = w_in[:, :o_k].astype(BF16)
    wqi = w_in[:, o_qi:o_ki].astype(BF16)
    wsm = jnp.concatenate([w_in[:, o_k:o_qi], w_in[:, o_ki:],
                           jnp.zeros((dm, LANES - IDX_DIM - IDX_HEADS), w_in.dtype)], axis=1).astype(BF16)
    full, halfs = _rope_tables(seq)
    tm = 256
    nt = seq // tm
    row = lambda i: (i, 0)
    tab = lambda i: (0, i % nt, 0)
    q, qi, k, v, ki, wi = pl.pallas_call(
        _dsa_proj_kernel,
        grid=(t // tm,),
        in_specs=[
            pl.BlockSpec((tm, dm), row), _resident((1, dm)), _resident(wq.shape), _resident(wqi.shape),
            _resident(wsm.shape),
            pl.BlockSpec((3, tm, LANES), tab), pl.BlockSpec((4, tm, LANES), tab),
        ],
        out_specs=[
            pl.BlockSpec((tm, dq), row), pl.BlockSpec((tm, dqi), row),
            pl.BlockSpec((tm, ATTN_HEAD_DIM), row), pl.BlockSpec((tm, ATTN_HEAD_DIM), row),
            pl.BlockSpec((tm, IDX_DIM), row), pl.BlockSpec((tm, LANES), row),
        ],
        out_shape=[
            jax.ShapeDtypeStruct((t, dq), BF16), jax.ShapeDtypeStruct((t, dqi), BF16),
            jax.ShapeDtypeStruct((t, ATTN_HEAD_DIM), BF16), jax.ShapeDtypeStruct((t, ATTN_HEAD_DIM), BF16),
            jax.ShapeDtypeStruct((t, IDX_DIM), BF16), jax.ShapeDtypeStruct((t, LANES), F32),
        ],
        compiler_params=_params("parallel"),
        name="dsa_proj",
    )(x, g_mix.reshape(1, dm), wq, wqi, wsm, full, halfs)

    qb = QUERY_BLOCK
    nb = seq // qb
    top_k = min(TOPK_MAX, seq // 4)
    blk = lambda b, n: (b * nb + n, 0)
    per_b = lambda b, n: (b, 0, 0)
    span_ends = tuple(sorted({e for e in (nb // 8, nb // 4, 3 * nb // 8, nb // 2, 5 * nb // 8, 3 * nb // 4, nb) if e > 0}))
    wi_t = wi[:, IDX_DIM:IDX_DIM + IDX_HEADS].reshape(bsz, seq, IDX_HEADS).transpose(0, 2, 1)
    v_t = v.reshape(bsz, seq, ATTN_HEAD_DIM).transpose(0, 2, 1)
    ones_rows = jnp.zeros((bsz, 2 * SUBLANES, seq), BF16).at[:, 0, :].set(1.0)
    v_t = jnp.concatenate([v_t, ones_rows], axis=1)
    o = pl.pallas_call(
        functools.partial(_dsa_attn_kernel, top_k=top_k, span_ends=span_ends),
        grid=(bsz, nb),
        in_specs=[
            pl.BlockSpec((qb, dq), blk), pl.BlockSpec((qb, dqi), blk),
            pl.BlockSpec((None, IDX_HEADS, qb), lambda b, n: (b, 0, n)),
            pl.BlockSpec((None, seq, ATTN_HEAD_DIM), per_b),
            pl.BlockSpec((None, ATTN_HEAD_DIM + 2 * SUBLANES, seq), per_b),
            pl.BlockSpec((None, seq, IDX_DIM), per_b),
        ],
        out_specs=pl.BlockSpec((qb, dq), blk),
        out_shape=jax.ShapeDtypeStruct((t, dq), BF16),
        scratch_shapes=[
            pltpu.VMEM((seq, qb), jnp.int32), pltpu.VMEM((seq, qb), F32),
            pltpu.VMEM((ATTN_HEADS // DSA_HEAD_GROUP, DSA_HEAD_GROUP * qb, ATTN_HEAD_DIM), BF16),
            pltpu.VMEM((IDX_HEADS // DSA_HEAD_GROUP, DSA_HEAD_GROUP * qb, IDX_DIM), BF16),
            pltpu.VMEM((ATTN_HEADS // DSA_HEAD_GROUP, qb, DSA_HEAD_GROUP * ATTN_HEAD_DIM), BF16),
            pltpu.VMEM((seq, DSA_HEAD_GROUP * qb), F32), pltpu.VMEM((seq, DSA_HEAD_GROUP * qb), F32),
        ],
        compiler_params=_params("parallel", "arbitrary"),
        name="dsa_attn",
    )(q, qi, wi_t, k.reshape(bsz, seq, -1), v_t, ki.reshape(bsz, seq, -1))

    tmo = 512
    return pl.pallas_call(
        _proj_res_kernel,
        grid=(t // tmo,),
        in_specs=[pl.BlockSpec((tmo, dq), row), _resident((dq, dm)), pl.BlockSpec((tmo, dm), row)],
        out_specs=pl.BlockSpec((tmo, dm), row),
        out_shape=jax.ShapeDtypeStruct((t, dm), F32),
        compiler_params=_params("parallel"),
        name="dsa_out",
    )(o, w_out.astype(BF16), x)


def kernel(x, ffn1_norm, ffn1_w_in, ffn1_w_out, mix_norm, ffn2_norm, ffn2_w_in, ffn2_w_out, ssd_in_proj, ssd_conv_w, ssd_conv_b, ssd_dt_bias, ssd_a_log, ssd_d, ssd_gate_norm, ssd_out_proj, s5_b_re, s5_b_im, s5_c_re, s5_c_im, s5_lam_re, s5_lam_im, s5_log_step, s5_d, s5_glu_w, s5_glu_b, dsa_in_proj, dsa_out_proj, final_norm):
    bsz, seq, dm = x.shape
    depth = ffn1_norm.shape[0]
    xt = x.reshape(bsz * seq, dm)
    for i in range(depth):
        j, kind = divmod(i, N_MIXERS)
        xt = _ffn_call(xt, ffn1_norm[i], ffn1_w_in[i].astype(BF16), ffn1_w_out[i].astype(BF16))
        if kind == 0:
            xt = _ssd_call(xt, bsz, seq, mix_norm[i], ssd_in_proj[j], ssd_conv_w[j], ssd_conv_b[j],
                           ssd_dt_bias[j], ssd_a_log[j], ssd_d[j], ssd_gate_norm[j], ssd_out_proj[j])
        elif kind == 1:
            xt = _s5_call(xt, bsz, seq, mix_norm[i], s5_b_re[j], s5_b_im[j], s5_c_re[j], s5_c_im[j],
                          s5_lam_re[j], s5_lam_im[j], s5_log_step[j], s5_d[j], s5_glu_w[j], s5_glu_b[j])
        else:
            xt = _dsa_call(xt, bsz, seq, mix_norm[i], dsa_in_proj[j], dsa_out_proj[j])
        post = final_norm if i == depth - 1 else None
        xt = _ffn_call(xt, ffn2_norm[i], ffn2_w_in[i].astype(BF16), ffn2_w_out[i].astype(BF16), post)
    return xt.reshape(bsz, seq, dm)
```
